```python
import numpy as np
import jax
import jax.numpy as jnp
from jax import lax

D_MODEL = 2048
BATCH = 2
SEQ = 16384
DEPTH = 1

PLE_DIM = 256
HEAD_DIM = 64
N_HEADS = D_MODEL // 128
N_KV_HEADS = max(1, N_HEADS // 8)
GQA_GROUP = N_HEADS // N_KV_HEADS
WINDOW = 128
M_HEADS = 4
M_DV = D_MODEL // (2 * M_HEADS)
M_DQK = M_DV // 2
M_CHUNK = 64
GATE_SOFTCAP = 15.0
N_EXPERTS = 64
TOP_K = 8
D_EXPERT = D_MODEL // 4
D_SHARED = D_MODEL // 4
ROUTED_SCALE = 2.5
MOE_BLOCK = 256
LN_EPS = 1e-5
RMS_EPS = 1e-6
DEEPNORM_ALPHA = (2.0 * DEPTH) ** 0.25
DEEPNORM_BETA = (8.0 * DEPTH) ** -0.25

ATTN_Q_W = N_HEADS * HEAD_DIM
ATTN_KV_W = N_KV_HEADS * HEAD_DIM
M_QK_W = M_HEADS * M_DQK
M_V_W = M_HEADS * M_DV
IN_SIZES = (ATTN_Q_W, ATTN_KV_W, ATTN_KV_W, M_QK_W, M_QK_W, M_V_W, M_V_W, M_HEADS, M_HEADS, D_MODEL, D_MODEL)
IN_COLS = sum(IN_SIZES)

kernel_name = 'hybrid_swa_mlstm_moe_deepnorm'


def layer_norm(x, g, b):
    xf = x.astype(jnp.float32)
    mu = jnp.mean(xf, axis=-1, keepdims=True)
    var = jnp.mean(jnp.square(xf - mu), axis=-1, keepdims=True)
    y = (xf - mu) * lax.rsqrt(var + LN_EPS) * g.astype(jnp.float32) + b.astype(jnp.float32)
    return y.astype(x.dtype)


def soft_cap(z):
    return GATE_SOFTCAP * jnp.tanh(z / GATE_SOFTCAP)


def alibi_slopes():
    return jnp.exp2(-8.0 / N_HEADS * jnp.arange(1, N_HEADS + 1, dtype=jnp.float32))


def swa_sink_attention(q, k, v, sinks):
    B, S = q.shape[0], q.shape[1]
    nb = S // WINDOW
    qb = q.reshape(B, nb, WINDOW, N_KV_HEADS, GQA_GROUP, HEAD_DIM)
    kb = k.reshape(B, nb, WINDOW, N_KV_HEADS, HEAD_DIM)
    vb = v.reshape(B, nb, WINDOW, N_KV_HEADS, HEAD_DIM)
    pad = ((0, 0), (1, 0), (0, 0), (0, 0), (0, 0))
    kk = jnp.concatenate([jnp.pad(kb[:, :-1], pad), kb], axis=2)
    vv = jnp.concatenate([jnp.pad(vb[:, :-1], pad), vb], axis=2)
    s = jnp.einsum('bnqhgd,bnkhd->bnhgqk', qb, kk).astype(jnp.float32) * (HEAD_DIM ** -0.5)
    qi = jnp.arange(WINDOW)[:, None]
    kj = jnp.arange(2 * WINDOW)[None, :]
    dist = qi - kj + WINDOW
    in_band = (dist >= 0) & (dist < WINDOW)
    key_pos = jnp.arange(nb)[:, None] * WINDOW - WINDOW + jnp.arange(2 * WINDOW)[None, :]
    mask = in_band[None] & (key_pos >= 0)[:, None, :]
    slopes = alibi_slopes().reshape(N_KV_HEADS, GQA_GROUP)
    bias = -slopes[:, :, None, None] * dist.astype(jnp.float32)
    s = jnp.where(mask[None, :, None, None], s + bias, -jnp.inf)
    sink = jnp.broadcast_to(sinks.astype(jnp.float32).reshape(N_KV_HEADS, GQA_GROUP, 1, 1), s.shape[:-1] + (1,))
    probs = jax.nn.softmax(jnp.concatenate([s, sink], axis=-1), axis=-1)[..., :-1]
    o = jnp.einsum('bnhgqk,bnkhd->bnqhgd', probs.astype(v.dtype), vv)
    return o.reshape(B, S, N_HEADS * HEAD_DIM)


def _to_chunks(a, nc):
    B = a.shape[0]
    a = a.reshape((B, nc, M_CHUNK) + a.shape[2:])
    perm = (1, 0, 3, 2) + tuple(range(4, a.ndim))
    return a.transpose(perm)


def mlstm_chunkwise(q, k, v, i_pre, f_pre, norm_g):
    B, S = q.shape[0], q.shape[1]
    nc = S // M_CHUNK
    f32 = jnp.float32
    qc = _to_chunks(q.astype(f32), nc)
    kc = _to_chunks(k.astype(f32) * (M_DQK ** -0.5), nc)
    vc = _to_chunks(v.astype(f32), nc)
    igc = _to_chunks(soft_cap(i_pre.astype(f32)), nc)
    lfc = _to_chunks(jax.nn.log_sigmoid(soft_cap(f_pre.astype(f32))), nc)
    causal = jnp.tril(jnp.ones((M_CHUNK, M_CHUNK), dtype=bool))

    def step(carry, inp):
        C, n, m = carry
        qx, kx, vx, ig, lf = inp
        b = jnp.cumsum(lf, axis=-1)
        inter = b + m[..., None]
        dmat = jnp.where(causal, b[..., :, None] - b[..., None, :] + ig[..., None, :], -jnp.inf)
        mt = jnp.maximum(inter, jnp.max(dmat, axis=-1))
        w = jnp.exp(dmat - mt[..., None])
        sqk = jnp.einsum('bhtd,bhsd->bhts', qx, kx) * w
        si = jnp.exp(inter - mt)
        num = si[..., None] * jnp.einsum('bhtd,bhdv->bhtv', qx, C) + jnp.einsum('bhts,bhsv->bhtv', sqk, vx)
        nq = si * jnp.einsum('bhtd,bhd->bht', qx, n) + jnp.sum(sqk, axis=-1)
        h = num / jnp.maximum(jnp.abs(nq), jnp.exp(-mt))[..., None]
        bl = b[..., -1]
        g = bl[..., None] - b + ig
        m_new = jnp.maximum(bl + m, jnp.max(g, axis=-1))
        decay = jnp.exp(bl + m - m_new)
        wg = jnp.exp(g - m_new[..., None])
        C_new = decay[..., None, None] * C + jnp.einsum('bhs,bhsd,bhsv->bhdv', wg, kx, vx)
        n_new = decay[..., None] * n + jnp.einsum('bhs,bhsd->bhd', wg, kx)
        return (C_new, n_new, m_new), h

    init = (jnp.zeros((B, M_HEADS, M_DQK, M_DV), f32), jnp.zeros((B, M_HEADS, M_DQK), f32), jnp.zeros((B, M_HEADS), f32))
    _, hs = lax.scan(step, init, (qc, kc, vc, igc, lfc))
    h = hs.transpose(1, 0, 3, 2, 4).reshape(B, S, M_HEADS, M_DV)
    h = h * lax.rsqrt(jnp.mean(jnp.square(h), axis=-1, keepdims=True) + RMS_EPS) * norm_g.astype(f32)
    return h.astype(q.dtype)


def moe_ffn(xf, w_router, b_router, w_eg, w_eu, w_ed, w_sg, w_su, w_sd):
    T, D = xf.shape
    scores = jax.nn.sigmoid((xf @ w_router).astype(jnp.float32))
    _, idx = lax.top_k(scores + b_router.astype(jnp.float32), TOP_K)
    gw = jnp.take_along_axis(scores, idx, axis=-1)
    gw = gw / jnp.sum(gw, axis=-1, keepdims=True) * ROUTED_SCALE
    e_flat = idx.reshape(-1)
    w_flat = gw.reshape(-1)
    tok_flat = jnp.arange(T * TOP_K, dtype=jnp.int32) // TOP_K
    order = jnp.argsort(e_flat)
    e_sorted = e_flat[order]
    counts = jnp.bincount(e_flat, length=N_EXPERTS)
    starts = jnp.cumsum(counts) - counts
    padded = (counts + MOE_BLOCK - 1) // MOE_BLOCK * MOE_BLOCK
    ends = jnp.cumsum(padded)
    pstarts = ends - padded
    dest = pstarts[e_sorted] + (jnp.arange(T * TOP_K) - starts[e_sorted])
    n_blocks = -(-(T * TOP_K) // MOE_BLOCK) + N_EXPERTS
    n_rows = n_blocks * MOE_BLOCK
    row_tok = jnp.full((n_rows,), T, dtype=jnp.int32).at[dest].set(tok_flat[order])
    row_w = jnp.zeros((n_rows,), jnp.float32).at[dest].set(w_flat[order])
    block_e = jnp.minimum(jnp.searchsorted(ends, jnp.arange(n_blocks) * MOE_BLOCK, side='right'), N_EXPERTS - 1)
    x_pad = jnp.concatenate([xf, jnp.zeros((1, D), xf.dtype)], axis=0)

    def body(acc, blk):
        toks, wts, e = blk
        xb = x_pad[toks]
        hb = jax.nn.silu(xb @ w_eg[e]) * (xb @ w_eu[e])
        yb = (hb @ w_ed[e]).astype(jnp.float32) * wts[:, None]
        return acc.at[toks].add(yb), None

    acc0 = jnp.zeros((T + 1, D), jnp.float32)
    acc, _ = lax.scan(body, acc0, (row_tok.reshape(n_blocks, MOE_BLOCK), row_w.reshape(n_blocks, MOE_BLOCK), block_e))
    shared = (jax.nn.silu(xf @ w_sg) * (xf @ w_su)) @ w_sd
    return (acc[:T] + shared.astype(jnp.float32)).astype(xf.dtype)


def hybrid_layer(x, p_i, w_in, sinks, b_i, b_f, norm_g, w_ba, w_bm, w_out, ln_mix_g, ln_mix_b,
                 w_router, b_router, w_eg, w_eu, w_ed, w_sg, w_su, w_sd, ln_ffn_g, ln_ffn_b,
                 w_pp, w_pg, ln_ple_g, ln_ple_b):
    B, S, D = x.shape
    proj = x @ w_in
    splits = np.cumsum(IN_SIZES)[:-1].tolist()
    aq, ak, av, mq, mk, mv, mo, mi, mf, ga, gb = jnp.split(proj, splits, axis=-1)
    attn = swa_sink_attention(aq.reshape(B, S, N_HEADS, HEAD_DIM), ak.reshape(B, S, N_KV_HEADS, HEAD_DIM),
                              av.reshape(B, S, N_KV_HEADS, HEAD_DIM), sinks)
    mh = mlstm_chunkwise(mq.reshape(B, S, M_HEADS, M_DQK), mk.reshape(B, S, M_HEADS, M_DQK),
                         mv.reshape(B, S, M_HEADS, M_DV), mi + b_i, mf + b_f, norm_g)
    mh = (mh * jax.nn.sigmoid(mo).reshape(B, S, M_HEADS, M_DV)).reshape(B, S, M_V_W)
    merged = jax.nn.sigmoid(ga) * (attn @ w_ba) + jax.nn.sigmoid(gb) * (mh @ w_bm)
    x = layer_norm(DEEPNORM_ALPHA * x + merged @ w_out, ln_mix_g, ln_mix_b)
    ffn = moe_ffn(x.reshape(B * S, D), w_router, b_router, w_eg, w_eu, w_ed, w_sg, w_su, w_sd).reshape(B, S, D)
    x = layer_norm(DEEPNORM_ALPHA * x + ffn, ln_ffn_g, ln_ffn_b)
    ple = (p_i @ w_pp) * jax.nn.sigmoid(x @ w_pg)
    x = layer_norm(DEEPNORM_ALPHA * x + ple, ln_ple_g, ln_ple_b)
    return x


def setup_inputs(seed: int = 0) -> dict:
    key = jax.random.key(seed)
    ks = jax.random.split(key, 32)
    f32 = jnp.float32

    def nrm(k, shape, scale):
        return jax.random.normal(k, shape, f32) * scale

    L, D, E = DEPTH, D_MODEL, N_EXPERTS
    return {
        'x': nrm(ks[0], (BATCH, SEQ, D), 1.0),
        'p': nrm(ks[1], (DEPTH, BATCH, SEQ, PLE_DIM), 1.0),
        'ln_in_g': 1.0 + nrm(ks[2], (D,), 0.02),
        'ln_in_b': nrm(ks[3], (D,), 0.02),
        'w_in': nrm(ks[4], (L, D, IN_COLS), D ** -0.5),
        'attn_sinks': nrm(ks[5], (L, N_HEADS), 1.0),
        'mlstm_b_i': nrm(ks[6], (L, M_HEADS), 0.1),
        'mlstm_b_f': 3.0 + nrm(ks[7], (L, M_HEADS), 0.5),
        'mlstm_norm_g': 1.0 + nrm(ks[8], (L, M_HEADS, M_DV), 0.02),
        'w_branch_attn': nrm(ks[9], (L, ATTN_Q_W, D), ATTN_Q_W ** -0.5),
        'w_branch_mlstm': nrm(ks[10], (L, M_V_W, D), M_V_W ** -0.5),
        'w_out': nrm(ks[11], (L, D, D), D ** -0.5 * DEEPNORM_BETA),
        'ln_mix_g': 1.0 + nrm(ks[12], (L, D), 0.02),
        'ln_mix_b': nrm(ks[13], (L, D), 0.02),
        'w_router': nrm(ks[14], (L, D, E), D ** -0.5),
        'b_router': nrm(ks[15], (L, E), 0.01),
        'w_exp_gate': nrm(ks[16], (L, E, D, D_EXPERT), D ** -0.5),
        'w_exp_up': nrm(ks[17], (L, E, D, D_EXPERT), D ** -0.5),
        'w_exp_down': nrm(ks[18], (L, E, D_EXPERT, D), D_EXPERT ** -0.5 * DEEPNORM_BETA),
        'w_sh_gate': nrm(ks[19], (L, D, D_SHARED), D ** -0.5),
        'w_sh_up': nrm(ks[20], (L, D, D_SHARED), D ** -0.5),
        'w_sh_down': nrm(ks[21], (L, D_SHARED, D), D_SHARED ** -0.5 * DEEPNORM_BETA),
        'ln_ffn_g': 1.0 + nrm(ks[22], (L, D), 0.02),
        'ln_ffn_b': nrm(ks[23], (L, D), 0.02),
        'w_ple_proj': nrm(ks[24], (L, PLE_DIM, D), PLE_DIM ** -0.5 * DEEPNORM_BETA),
        'w_ple_gate': nrm(ks[25], (L, D, D), D ** -0.5),
        'ln_ple_g': 1.0 + nrm(ks[26], (L, D), 0.02),
        'ln_ple_b': nrm(ks[27], (L, D), 0.02),
    }


def reference(x, p, ln_in_g, ln_in_b, w_in, attn_sinks, mlstm_b_i, mlstm_b_f, mlstm_norm_g,
              w_branch_attn, w_branch_mlstm, w_out, ln_mix_g, ln_mix_b, w_router, b_router,
              w_exp_gate, w_exp_up, w_exp_down, w_sh_gate, w_sh_up, w_sh_down, ln_ffn_g, ln_ffn_b,
              w_ple_proj, w_ple_gate, ln_ple_g, ln_ple_b):
    h = layer_norm(x, ln_in_g, ln_in_b)
    for i in range(DEPTH):
        h = hybrid_layer(h, p[i], w_in[i], attn_sinks[i], mlstm_b_i[i], mlstm_b_f[i], mlstm_norm_g[i],
                         w_branch_attn[i], w_branch_mlstm[i], w_out[i], ln_mix_g[i], ln_mix_b[i],
                         w_router[i], b_router[i], w_exp_gate[i], w_exp_up[i], w_exp_down[i],
                         w_sh_gate[i], w_sh_up[i], w_sh_down[i], ln_ffn_g[i], ln_ffn_b[i],
                         w_ple_proj[i], w_ple_gate[i], ln_ple_g[i], ln_ple_b[i])
    return h
```

```python
import functools

import numpy as np
import jax
import jax.numpy as jnp
from jax import lax
from jax.experimental import pallas as pl
from jax.experimental.pallas import tpu as pltpu

F32 = jnp.float32
BF16 = jnp.bfloat16

D_MODEL = 2048
PLE_DIM = 256
HEAD_DIM = 64
N_HEADS = 16
N_KV_HEADS = 2
GQA_GROUP = N_HEADS // N_KV_HEADS
WINDOW = 128
M_HEADS = 4
M_DV = 256
M_DQK = 128
M_CHUNK = 64
GATE_SOFTCAP = 15.0
N_EXPERTS = 64
TOP_K = 8
D_EXPERT = 512
D_SHARED = 512
ROUTED_SCALE = 2.5
MOE_BLOCK = 256
LN_EPS = 1e-5
RMS_EPS = 1e-6
DEPTH = 1
DEEPNORM_ALPHA = (2.0 * DEPTH) ** 0.25

ATTN_Q_W = N_HEADS * HEAD_DIM
ATTN_KV_W = N_KV_HEADS * HEAD_DIM
M_QK_W = M_HEADS * M_DQK
M_V_W = M_HEADS * M_DV

LANES = 128
NEG_BIG = -1e30

C_GA = 0
C_GB = C_GA + D_MODEL
C_AQ = C_GB + D_MODEL
C_MV = C_AQ + ATTN_Q_W
C_MO = C_MV + M_V_W
C_MQ = C_MO + M_V_W
C_MK = C_MQ + M_QK_W
C_AK = C_MK + M_QK_W
C_AV = C_AK + ATTN_KV_W
C_IF = C_AV + ATTN_KV_W
PROJ_TN = 512
PROJ_COLS = 8704
IF_TILE = C_IF // PROJ_TN
IF_OFF = C_IF - IF_TILE * PROJ_TN

VMEM_LIMIT = 52 * 1024 * 1024


def _cparams(n_axes):
    return pltpu.CompilerParams(dimension_semantics=("arbitrary",) * n_axes, vmem_limit_bytes=VMEM_LIMIT)


def _layer_norm(x, g, b):
    mu = jnp.mean(x, axis=-1, keepdims=True)
    xc = x - mu
    var = jnp.mean(xc * xc, axis=-1, keepdims=True)
    return xc * lax.rsqrt(var + LN_EPS) * g + b


def _dot(a, b):
    return jnp.dot(a, b, preferred_element_type=F32)


def _dot_nt(a, b):
    return lax.dot_general(a, b, (((1,), (1,)), ((), ())), preferred_element_type=F32)


def _inproj_kernel(x_ref, g_ref, b_ref, w_ref, o_ref, og_ref, xn_ref):
    j = pl.program_id(1)

    @pl.when(j == 0)
    def _():
        xn_ref[...] = _layer_norm(x_ref[...], g_ref[...], b_ref[...]).astype(BF16)

    acc = _dot(xn_ref[...], w_ref[...])
    o_ref[...] = acc.astype(o_ref.dtype)

    @pl.when(j == IF_TILE)
    def _():
        og_ref[...] = acc[:, IF_OFF:IF_OFF + LANES]


def _in_projection(x2, ln_g, ln_b, w_perm, tm):
    T, D = x2.shape
    return pl.pallas_call(
        _inproj_kernel,
        grid=(T // tm, PROJ_COLS // PROJ_TN),
        in_specs=[
            pl.BlockSpec((tm, D), lambda i, j: (i, 0)),
            pl.BlockSpec((1, D), lambda i, j: (0, 0)),
            pl.BlockSpec((1, D), lambda i, j: (0, 0)),
            pl.BlockSpec((D, PROJ_TN), lambda i, j: (0, j)),
        ],
        out_specs=[
            pl.BlockSpec((tm, PROJ_TN), lambda i, j: (i, j)),
            pl.BlockSpec((tm, LANES), lambda i, j: (i, 0)),
        ],
        out_shape=[
            jax.ShapeDtypeStruct((T, PROJ_COLS), BF16),
            jax.ShapeDtypeStruct((T, LANES), F32),
        ],
        scratch_shapes=[pltpu.VMEM((tm, D), BF16)],
        compiler_params=_cparams(2),
        name="ln_inproj",
    )(x2, ln_g, ln_b, w_perm)


def _alibi_slope(h):
    return float(2.0 ** (-8.0 / N_HEADS * (h + 1)))


def _attn_kernel(sink_ref, q_ref, kp_ref, kc_ref, vp_ref, vc_ref, o_ref, *, blocks_per_seq):
    n = pl.program_id(0) % blocks_per_seq
    W = WINDOW
    qi = lax.broadcasted_iota(jnp.int32, (W, 2 * W), 0)
    kj = lax.broadcasted_iota(jnp.int32, (W, 2 * W), 1)
    dist = qi - kj + W
    valid = (dist >= 0) & (dist < W) & ((kj >= W) | (n > 0))
    neg_dist = jnp.where(valid, -dist.astype(F32), NEG_BIG)

    lane = lax.broadcasted_iota(jnp.int32, (2 * W, LANES), 1)
    lo = lane < HEAD_DIM
    lo_w = lax.broadcasted_iota(jnp.int32, (W, LANES), 1) < HEAD_DIM

    def dup_halves(prev_ref, cur_ref, g):
        a = jnp.concatenate([prev_ref[...], cur_ref[...]], axis=0).astype(F32)
        r = pltpu.roll(a, HEAD_DIM, 1)
        return jnp.where(lo, a, r) if g == 0 else jnp.where(lo, r, a)

    for g in range(N_KV_HEADS):
        kd = dup_halves(kp_ref, kc_ref, g)
        vd = dup_halves(vp_ref, vc_ref, g)
        zero = jnp.zeros_like(kd)
        kbd = jnp.concatenate([jnp.where(lo, kd, zero), jnp.where(lo, zero, kd)], axis=0).astype(BF16)
        vbd = jnp.concatenate([jnp.where(lo, vd, zero), jnp.where(lo, zero, vd)], axis=0).astype(BF16)
        for pp in range(GQA_GROUP // 2):
            pair = g * (GQA_GROUP // 2) + pp
            h_e, h_o = 2 * pair, 2 * pair + 1
            q2 = q_ref[:, pair * LANES:(pair + 1) * LANES]
            s = _dot_nt(q2, kbd)
            outs = []
            for half, h in ((0, h_e), (1, h_o)):
                sh = s[:, half * 2 * W:(half + 1) * 2 * W] + _alibi_slope(h) * neg_dist
                sink = sink_ref[h]
                m = jnp.maximum(jnp.max(sh, axis=1, keepdims=True), sink)
                p = jnp.exp(sh - m)
                l = jnp.sum(p, axis=1, keepdims=True) + jnp.exp(sink - m)
                outs.append((p, l))
            p_all = jnp.concatenate([outs[0][0], outs[1][0]], axis=1).astype(BF16)
            o2 = _dot(p_all, vbd)
            inv = jnp.where(lo_w, 1.0 / outs[0][1], 1.0 / outs[1][1])
            o_ref[:, pair * LANES:(pair + 1) * LANES] = (o2 * inv).astype(o_ref.dtype)


def _attention(proj, sinks, B, S):
    T = B * S
    nb = S // WINDOW
    kcol = C_AK // LANES
    vcol = C_AV // LANES
    kern = functools.partial(_attn_kernel, blocks_per_seq=nb)
    grid_spec = pltpu.PrefetchScalarGridSpec(
        num_scalar_prefetch=1,
        grid=(T // WINDOW,),
        in_specs=[
            pl.BlockSpec((WINDOW, ATTN_Q_W), lambda i, s: (i, C_AQ // ATTN_Q_W)),
            pl.BlockSpec((WINDOW, LANES), lambda i, s: (jnp.maximum(i - 1, 0), kcol)),
            pl.BlockSpec((WINDOW, LANES), lambda i, s: (i, kcol)),
            pl.BlockSpec((WINDOW, LANES), lambda i, s: (jnp.maximum(i - 1, 0), vcol)),
            pl.BlockSpec((WINDOW, LANES), lambda i, s: (i, vcol)),
        ],
        out_specs=pl.BlockSpec((WINDOW, ATTN_Q_W), lambda i, s: (i, 0)),
    )
    return pl.pallas_call(
        kern,
        grid_spec=grid_spec,
        out_shape=jax.ShapeDtypeStruct((T, ATTN_Q_W), BF16),
        compiler_params=_cparams(1),
        name="swa_attention",
    )(sinks, proj, proj, proj, proj, proj)


def _soft_cap(z):
    return GATE_SOFTCAP * jnp.tanh(z / GATE_SOFTCAP)


def _mlstm_kernel(q_ref, k_ref, v_ref, og_ref, gates_ref, gbias_ref, normg_ref, o_ref, c_ref, n_ref, m_ref):
    L = M_CHUNK

    @pl.when(pl.program_id(1) == 0)
    def _():
        c_ref[...] = jnp.zeros_like(c_ref)
        n_ref[...] = jnp.zeros_like(n_ref)
        m_ref[...] = jnp.zeros_like(m_ref)

    sc = _soft_cap(gates_ref[...] + gbias_ref[...])
    lf = jax.nn.log_sigmoid(sc)
    ti = lax.broadcasted_iota(jnp.int32, (L, L), 0)
    si = lax.broadcasted_iota(jnp.int32, (L, L), 1)
    causal = si <= ti
    tril = causal.astype(F32)
    bcum = jnp.dot(tril, lf, precision=lax.Precision.HIGHEST, preferred_element_type=F32)
    sc_t = sc.T
    bcum_t = bcum.T

    for h in range(M_HEADS):
        ig_col = sc[:, h:h + 1]
        ig_row = sc_t[h:h + 1, :]
        b_col = bcum[:, M_HEADS + h:M_HEADS + h + 1]
        b_row = bcum_t[M_HEADS + h:M_HEADS + h + 1, :]
        m_prev = m_ref[h][:, :1]
        C = c_ref[h]
        nvec = n_ref[h]

        qx = q_ref[:, h * M_DQK:(h + 1) * M_DQK]
        kx = k_ref[:, h * M_DQK:(h + 1) * M_DQK]
        vx = v_ref[:, h * M_DV:(h + 1) * M_DV]

        inter = b_col + m_prev
        dmat = jnp.where(causal, b_col - b_row + ig_row, NEG_BIG)
        mt = jnp.maximum(inter, jnp.max(dmat, axis=1, keepdims=True))
        w = jnp.exp(dmat - mt)
        sqk = _dot_nt(qx, kx) * w
        s_i = jnp.exp(inter - mt)
        num = s_i * _dot(qx, C.astype(BF16)) + _dot(sqk.astype(BF16), vx)
        nq = s_i * jnp.sum(qx.astype(F32) * nvec, axis=1, keepdims=True) + jnp.sum(sqk, axis=1, keepdims=True)
        hh = num / jnp.maximum(jnp.abs(nq), jnp.exp(-mt))

        bl = b_col[L - 1:L, :]
        g_col = bl - b_col + ig_col
        m_new = jnp.maximum(bl + m_prev, jnp.max(g_col, axis=0, keepdims=True))
        decay = jnp.exp(bl + m_prev - m_new)
        wg = jnp.exp(g_col - m_new)
        kw = kx.astype(F32) * wg
        c_ref[h] = decay * C + _dot(kw.T.astype(BF16), vx)
        n_ref[h] = decay * nvec + jnp.sum(kw, axis=0, keepdims=True)
        m_ref[h] = jnp.broadcast_to(m_new, (1, LANES))

        hn = hh * lax.rsqrt(jnp.mean(hh * hh, axis=1, keepdims=True) + RMS_EPS) * normg_ref[h]
        og = og_ref[:, h * M_DV:(h + 1) * M_DV].astype(F32)
        o_ref[:, h * M_DV:(h + 1) * M_DV] = (hn * jax.nn.sigmoid(og)).astype(o_ref.dtype)


def _mlstm(proj, gates, gbias, norm_g, B, S):
    T = B * S
    nc = S // M_CHUNK
    L = M_CHUNK
    row = lambda b, c: b * nc + c
    return pl.pallas_call(
        _mlstm_kernel,
        grid=(B, nc),
        in_specs=[
            pl.BlockSpec((L, M_QK_W), lambda b, c: (row(b, c), C_MQ // M_QK_W)),
            pl.BlockSpec((L, M_QK_W), lambda b, c: (row(b, c), C_MK // M_QK_W)),
            pl.BlockSpec((L, M_V_W), lambda b, c: (row(b, c), C_MV // M_V_W)),
            pl.BlockSpec((L, M_V_W), lambda b, c: (row(b, c), C_MO // M_V_W)),
            pl.BlockSpec((L, LANES), lambda b, c: (row(b, c), 0)),
            pl.BlockSpec((1, LANES), lambda b, c: (0, 0)),
            pl.BlockSpec((M_HEADS, 1, M_DV), lambda b, c: (0, 0, 0)),
        ],
        out_specs=pl.BlockSpec((L, M_V_W), lambda b, c: (row(b, c), 0)),
        out_shape=jax.ShapeDtypeStruct((T, M_V_W), BF16),
        scratch_shapes=[
            pltpu.VMEM((M_HEADS, M_DQK, M_DV), F32),
            pltpu.VMEM((M_HEADS, 1, M_DQK), F32),
            pltpu.VMEM((M_HEADS, 1, LANES), F32),
        ],
        compiler_params=_cparams(2),
        name="mlstm_chunkwise",
    )(proj, proj, proj, proj, gates, gbias, norm_g)


def _mix_kernel(attn_ref, mh_ref, ga_ref, gb_ref, x_ref, lng_ref, lnb_ref, wba_ref, wbm_ref, wout_ref,
                mg_ref, mb_ref, o_ref, ob_ref):
    a = _dot(attn_ref[...], wba_ref[...])
    m = _dot(mh_ref[...], wbm_ref[...])
    merged = jax.nn.sigmoid(ga_ref[...].astype(F32)) * a + jax.nn.sigmoid(gb_ref[...].astype(F32)) * m
    y = _dot(merged.astype(BF16), wout_ref[...])
    h0 = _layer_norm(x_ref[...], lng_ref[...], lnb_ref[...])
    x1 = _layer_norm(DEEPNORM_ALPHA * h0 + y, mg_ref[...], mb_ref[...])
    o_ref[...] = x1
    ob_ref[...] = x1.astype(BF16)


def _const_spec(shape):
    nd = len(shape)
    return pl.BlockSpec(shape, lambda *_: (0,) * nd, pipeline_mode=pl.Buffered(1))


def _mix(attn, mh, proj, x2, ln_g, ln_b, w_ba, w_bm, w_out, mix_g, mix_b, tm):
    T, D = x2.shape
    return pl.pallas_call(
        _mix_kernel,
        grid=(T // tm,),
        in_specs=[
            pl.BlockSpec((tm, ATTN_Q_W), lambda i: (i, 0)),
            pl.BlockSpec((tm, M_V_W), lambda i: (i, 0)),
            pl.BlockSpec((tm, D), lambda i: (i, C_GA // D_MODEL)),
            pl.BlockSpec((tm, D), lambda i: (i, C_GB // D_MODEL)),
            pl.BlockSpec((tm, D), lambda i: (i, 0)),
            _const_spec((1, D)), _const_spec((1, D)),
            _const_spec((ATTN_Q_W, D)), _const_spec((M_V_W, D)), _const_spec((D, D)),
            _const_spec((1, D)), _const_spec((1, D)),
        ],
        out_specs=[pl.BlockSpec((tm, D), lambda i: (i, 0)), pl.BlockSpec((tm, D), lambda i: (i, 0))],
        out_shape=[jax.ShapeDtypeStruct((T, D), F32), jax.ShapeDtypeStruct((T, D), BF16)],
        compiler_params=_cparams(1),
        name="mix_outproj_ln",
    )(attn, mh, proj, proj, x2, ln_g, ln_b, w_ba, w_bm, w_out, mix_g, mix_b)


def _router_kernel(x_ref, wr_ref, br_ref, idx_ref, gw_ref):
    logits = jnp.dot(x_ref[...], wr_ref[...], precision=lax.Precision.HIGHEST, preferred_element_type=F32)
    scores = jax.nn.sigmoid(logits)
    sel = scores + br_ref[...]
    tm = scores.shape[0]
    lane = lax.broadcasted_iota(jnp.int32, (tm, N_EXPERTS), 1).astype(F32)
    out_lane = lax.broadcasted_iota(jnp.int32, (tm, LANES), 1)
    idx_out = jnp.zeros((tm, LANES), F32)
    gw_out = jnp.zeros((tm, LANES), F32)
    for k in range(TOP_K):
        mx = jnp.max(sel, axis=1, keepdims=True)
        am = jnp.min(jnp.where(sel == mx, lane, float(N_EXPERTS)), axis=1, keepdims=True)
        hit = lane == am
        gk = jnp.sum(jnp.where(hit, scores, 0.0), axis=1, keepdims=True)
        sel = jnp.where(hit, -jnp.inf, sel)
        idx_out = jnp.where(out_lane == k, am, idx_out)
        gw_out = jnp.where(out_lane == k, gk, gw_out)
    gw_out = gw_out / jnp.sum(gw_out, axis=1, keepdims=True) * ROUTED_SCALE
    idx_ref[...] = idx_out.astype(jnp.int32)
    gw_ref[...] = gw_out


def _router(x1, w_router, b_router, tm):
    T, D = x1.shape
    return pl.pallas_call(
        _router_kernel,
        grid=(T // tm,),
        in_specs=[
            pl.BlockSpec((tm, D), lambda i: (i, 0)),
            _const_spec((D, N_EXPERTS)),
            _const_spec((1, N_EXPERTS)),
        ],
        out_specs=[pl.BlockSpec((tm, LANES), lambda i: (i, 0)), pl.BlockSpec((tm, LANES), lambda i: (i, 0))],
        out_shape=[jax.ShapeDtypeStruct((T, LANES), jnp.int32), jax.ShapeDtypeStruct((T, LANES), F32)],
        compiler_params=_cparams(1),
        name="router_topk",
    )(x1, w_router, b_router)


def _expert_kernel(be_ref, nused_ref, xs_ref, rw_ref, wg_ref, wu_ref, wd_ref, y_ref):
    i = pl.program_id(0)

    @pl.when(i < nused_ref[0])
    def _():
        xb = xs_ref[...]
        hb = jax.nn.silu(_dot(xb, wg_ref[0])) * _dot(xb, wu_ref[0])
        y = _dot(hb.astype(BF16), wd_ref[0]) * rw_ref[...]
        y_ref[...] = y.astype(y_ref.dtype)

    @pl.when(i >= nused_ref[0])
    def _():
        y_ref[...] = jnp.zeros_like(y_ref)


def _experts(xs, row_w, block_e, n_used, w_eg, w_eu, w_ed):
    n_rows, D = xs.shape
    n_blocks = n_rows // MOE_BLOCK
    grid_spec = pltpu.PrefetchScalarGridSpec(
        num_scalar_prefetch=2,
        grid=(n_blocks,),
        in_specs=[
            pl.BlockSpec((MOE_BLOCK, D), lambda i, be, nu: (i, 0)),
            pl.BlockSpec((MOE_BLOCK, 1), lambda i, be, nu: (i, 0)),
            pl.BlockSpec((1, D, D_EXPERT), lambda i, be, nu: (be[i], 0, 0)),
            pl.BlockSpec((1, D, D_EXPERT), lambda i, be, nu: (be[i], 0, 0)),
            pl.BlockSpec((1, D_EXPERT, D), lambda i, be, nu: (be[i], 0, 0)),
        ],
        out_specs=pl.BlockSpec((MOE_BLOCK, D), lambda i, be, nu: (i, 0)),
    )
    return pl.pallas_call(
        _expert_kernel,
        grid_spec=grid_spec,
        out_shape=jax.ShapeDtypeStruct((n_rows, D), BF16),
        compiler_params=_cparams(1),
        name="expert_mlp",
    )(block_e, n_used, xs, row_w, w_eg, w_eu, w_ed)


def _ffn_ple_kernel(x1_ref, yr_ref, p_ref, wsg_ref, wsu_ref, wsd_ref, fg_ref, fb_ref, wpp_ref, wpg_ref,
                    pg_ref, pb_ref, o_ref):
    x1 = x1_ref[...]
    xb = x1.astype(BF16)
    hs = jax.nn.silu(_dot(xb, wsg_ref[...])) * _dot(xb, wsu_ref[...])
    shared = _dot(hs.astype(BF16), wsd_ref[...])
    x2 = _layer_norm(DEEPNORM_ALPHA * x1 + (yr_ref[...] + shared), fg_ref[...], fb_ref[...])
    ple = _dot(p_ref[...].astype(BF16), wpp_ref[...]) * jax.nn.sigmoid(_dot(x2.astype(BF16), wpg_ref[...]))
    o_ref[...] = _layer_norm(DEEPNORM_ALPHA * x2 + ple, pg_ref[...], pb_ref[...])


def _ffn_ple(x1, y_routed, p2, w_sg, w_su, w_sd, ffn_g, ffn_b, w_pp, w_pg, ple_g, ple_b, tm):
    T, D = x1.shape
    return pl.pallas_call(
        _ffn_ple_kernel,
        grid=(T // tm,),
        in_specs=[
            pl.BlockSpec((tm, D), lambda i: (i, 0)),
            pl.BlockSpec((tm, D), lambda i: (i, 0)),
            pl.BlockSpec((tm, PLE_DIM), lambda i: (i, 0)),
            _const_spec((D, D_SHARED)), _const_spec((D, D_SHARED)), _const_spec((D_SHARED, D)),
            _const_spec((1, D)), _const_spec((1, D)),
            _const_spec((PLE_DIM, D)), _const_spec((D, D)),
            _const_spec((1, D)), _const_spec((1, D)),
        ],
        out_specs=pl.BlockSpec((tm, D), lambda i: (i, 0)),
        out_shape=jax.ShapeDtypeStruct((T, D), F32),
        compiler_params=_cparams(1),
        name="shared_ffn_ple_ln",
    )(x1, y_routed, p2, w_sg, w_su, w_sd, ffn_g, ffn_b, w_pp, w_pg, ple_g, ple_b)


def _permute_w_in(w_in):
    D = w_in.shape[0]
    sizes = (ATTN_Q_W, ATTN_KV_W, ATTN_KV_W, M_QK_W, M_QK_W, M_V_W, M_V_W, M_HEADS, M_HEADS, D_MODEL, D_MODEL)
    offs = np.concatenate([[0], np.cumsum(sizes)])
    seg = [w_in[:, offs[k]:offs[k + 1]] for k in range(len(sizes))]
    aq, ak, av, mq, mk, mv, mo, mi, mf, ga, gb = seg
    aq = aq * (HEAD_DIM ** -0.5)
    mk = mk * (M_DQK ** -0.5)
    gates = jnp.concatenate([mi, mf, jnp.zeros((D, LANES - 2 * M_HEADS), w_in.dtype)], axis=1)
    pad = jnp.zeros((D, PROJ_COLS - (C_IF + LANES)), w_in.dtype)
    return jnp.concatenate([ga, gb, aq, mv, mo, mq, mk, ak, av, gates, pad], axis=1).astype(BF16)


def _route_metadata(idx, gw, T):
    e_flat = idx.reshape(-1)
    w_flat = gw.reshape(-1)
    n_pairs = T * TOP_K
    tok_flat = jnp.arange(n_pairs, dtype=jnp.int32) // TOP_K
    order = jnp.argsort(e_flat)
    e_sorted = e_flat[order]
    counts = jnp.bincount(e_flat, length=N_EXPERTS)
    starts = jnp.cumsum(counts) - counts
    padded = (counts + MOE_BLOCK - 1) // MOE_BLOCK * MOE_BLOCK
    ends = jnp.cumsum(padded)
    pstarts = ends - padded
    dest_sorted = (pstarts[e_sorted] + (jnp.arange(n_pairs) - starts[e_sorted])).astype(jnp.int32)
    n_blocks = -(-n_pairs // MOE_BLOCK) + N_EXPERTS
    n_rows = n_blocks * MOE_BLOCK
    row_tok = jnp.full((n_rows,), T, dtype=jnp.int32).at[dest_sorted].set(tok_flat[order])
    row_w = jnp.zeros((n_rows,), F32).at[dest_sorted].set(w_flat[order])
    block_e = jnp.minimum(jnp.searchsorted(ends, jnp.arange(n_blocks) * MOE_BLOCK, side='right'),
                          N_EXPERTS - 1).astype(jnp.int32)
    dest = jnp.zeros((n_pairs,), jnp.int32).at[order].set(dest_sorted)
    n_used = (ends[-1] // MOE_BLOCK).astype(jnp.int32).reshape(1)
    return row_tok, row_w, block_e, dest.reshape(T, TOP_K), n_used


def kernel(x, p, ln_in_g, ln_in_b, w_in, attn_sinks, mlstm_b_i, mlstm_b_f, mlstm_norm_g, w_branch_attn, w_branch_mlstm, w_out, ln_mix_g, ln_mix_b, w_router, b_router, w_exp_gate, w_exp_up, w_exp_down, w_sh_gate, w_sh_up, w_sh_down, ln_ffn_g, ln_ffn_b, w_ple_proj, w_ple_gate, ln_ple_g, ln_ple_b):
    B, S, D = x.shape
    T = B * S
    assert D == D_MODEL and S % WINDOW == 0 and S % M_CHUNK == 0 and w_in.shape[0] == DEPTH
    x2 = x.reshape(T, D)
    row = lambda v: v.reshape(1, -1).astype(F32)

    tm_proj = min(1024, T)
    tm_mix = min(256, T)
    tm_router = min(512, T)

    proj, gates = _in_projection(x2, row(ln_in_g), row(ln_in_b), _permute_w_in(w_in[0]), tm_proj)

    attn = _attention(proj, attn_sinks[0].astype(F32), B, S)

    gbias = jnp.concatenate([mlstm_b_i[0], mlstm_b_f[0], jnp.zeros((LANES - 2 * M_HEADS,), F32)]).reshape(1, LANES)
    mh = _mlstm(proj, gates, gbias, mlstm_norm_g[0].reshape(M_HEADS, 1, M_DV).astype(F32), B, S)

    x1, x1b = _mix(attn, mh, proj, x2, row(ln_in_g), row(ln_in_b),
                   w_branch_attn[0].astype(BF16), w_branch_mlstm[0].astype(BF16), w_out[0].astype(BF16),
                   row(ln_mix_g[0]), row(ln_mix_b[0]), tm_mix)

    idx128, gw128 = _router(x1, w_router[0].astype(F32), row(b_router[0]), tm_router)
    idx = idx128[:, :TOP_K]
    gw = gw128[:, :TOP_K]

    row_tok, row_w, block_e, dest, n_used = _route_metadata(idx, gw, T)
    x_pad = jnp.concatenate([x1b, jnp.zeros((1, D), BF16)], axis=0)
    xs = x_pad[row_tok]
    y = _experts(xs, row_w.reshape(-1, 1), block_e, n_used,
                 w_exp_gate[0].astype(BF16), w_exp_up[0].astype(BF16), w_exp_down[0].astype(BF16))
    y_routed = jnp.sum(y[dest.reshape(-1)].reshape(T, TOP_K, D).astype(F32), axis=1)

    out = _ffn_ple(x1, y_routed, p[0].reshape(T, PLE_DIM),
                   w_sh_gate[0].astype(BF16), w_sh_up[0].astype(BF16), w_sh_down[0].astype(BF16),
                   row(ln_ffn_g[0]), row(ln_ffn_b[0]),
                   w_ple_proj[0].astype(BF16), w_ple_gate[0].astype(BF16),
                   row(ln_ple_g[0]), row(ln_ple_b[0]), tm_mix)
    return out.reshape(B, S, D)
```

```python
import functools

import numpy as np
import jax
import jax.numpy as jnp
from jax import lax
from jax.experimental import pallas as pl
from jax.experimental.pallas import tpu as pltpu

F32 = jnp.float32
BF16 = jnp.bfloat16

D_MODEL = 2048
PLE_DIM = 256
HEAD_DIM = 64
N_HEADS = 16
N_KV_HEADS = 2
GQA_GROUP = N_HEADS // N_KV_HEADS
WINDOW = 128
M_HEADS = 4
M_DV = 256
M_DQK = 128
M_CHUNK = 64
GATE_SOFTCAP = 15.0
N_EXPERTS = 64
TOP_K = 8
D_EXPERT = 512
D_SHARED = 512
ROUTED_SCALE = 2.5
MOE_BLOCK = 256
LN_EPS = 1e-5
RMS_EPS = 1e-6
DEPTH = 1
DEEPNORM_ALPHA = (2.0 * DEPTH) ** 0.25

ATTN_Q_W = N_HEADS * HEAD_DIM
ATTN_KV_W = N_KV_HEADS * HEAD_DIM
M_QK_W = M_HEADS * M_DQK
M_V_W = M_HEADS * M_DV

LANES = 128
NEG_BIG = -1e30

C_GA = 0
C_GB = C_GA + D_MODEL
C_AQ = C_GB + D_MODEL
C_MV = C_AQ + ATTN_Q_W
C_MO = C_MV + M_V_W
C_MQ = C_MO + M_V_W
C_MK = C_MQ + M_QK_W
C_AK = C_MK + M_QK_W
C_AV = C_AK + ATTN_KV_W
C_IF = C_AV + ATTN_KV_W
PROJ_TN = 512
PROJ_COLS = 8704
IF_TILE = C_IF // PROJ_TN
IF_OFF = C_IF - IF_TILE * PROJ_TN

HALF_D = D_MODEL // 2
HI16 = 0xFFFF0000

VMEM_LIMIT = 52 * 1024 * 1024


def _cparams(n_axes):
    return pltpu.CompilerParams(dimension_semantics=("arbitrary",) * n_axes, vmem_limit_bytes=VMEM_LIMIT)


def _const_spec(shape):
    nd = len(shape)
    return pl.BlockSpec(shape, lambda *_: (0,) * nd, pipeline_mode=pl.Buffered(1))


def _layer_norm(x, g, b):
    mu = jnp.mean(x, axis=-1, keepdims=True)
    xc = x - mu
    var = jnp.mean(xc * xc, axis=-1, keepdims=True)
    return xc * lax.rsqrt(var + LN_EPS) * g + b


def _dot(a, b):
    return jnp.dot(a, b, preferred_element_type=F32)


def _dot_nt(a, b):
    return lax.dot_general(a, b, (((1,), (1,)), ((), ())), preferred_element_type=F32)


def _pack_bf16_pair(a, b):
    ab = lax.bitcast_convert_type(a.astype(BF16).astype(F32), jnp.uint32)
    bb = lax.bitcast_convert_type(b.astype(BF16).astype(F32), jnp.uint32)
    return (ab >> 16) | (bb & jnp.uint32(HI16))


def _unpack_bf16_pair(w):
    lo = lax.bitcast_convert_type(w << 16, F32)
    hi = lax.bitcast_convert_type(w & jnp.uint32(HI16), F32)
    return lo, hi


def _inproj_kernel(x_ref, g_ref, b_ref, w_ref, o_ref, og_ref, xn_ref):
    j = pl.program_id(1)

    @pl.when(j == 0)
    def _():
        xn_ref[...] = _layer_norm(x_ref[...], g_ref[...], b_ref[...]).astype(BF16)

    acc = _dot(xn_ref[...], w_ref[...])
    o_ref[...] = acc.astype(o_ref.dtype)

    @pl.when(j == IF_TILE)
    def _():
        og_ref[...] = acc[:, IF_OFF:IF_OFF + LANES]


def _in_projection(x2, ln_g, ln_b, w_perm, tm):
    T, D = x2.shape
    return pl.pallas_call(
        _inproj_kernel,
        grid=(T // tm, PROJ_COLS // PROJ_TN),
        in_specs=[
            pl.BlockSpec((tm, D), lambda i, j: (i, 0)),
            pl.BlockSpec((1, D), lambda i, j: (0, 0)),
            pl.BlockSpec((1, D), lambda i, j: (0, 0)),
            pl.BlockSpec((D, PROJ_TN), lambda i, j: (0, j)),
        ],
        out_specs=[
            pl.BlockSpec((tm, PROJ_TN), lambda i, j: (i, j)),
            pl.BlockSpec((tm, LANES), lambda i, j: (i, 0)),
        ],
        out_shape=[
            jax.ShapeDtypeStruct((T, PROJ_COLS), BF16),
            jax.ShapeDtypeStruct((T, LANES), F32),
        ],
        scratch_shapes=[pltpu.VMEM((tm, D), BF16)],
        compiler_params=_cparams(2),
        name="ln_inproj",
    )(x2, ln_g, ln_b, w_perm)


def _alibi_slope(h):
    return float(2.0 ** (-8.0 / N_HEADS * (h + 1)))


def _attn_kernel(sink_ref, q_ref, kp_ref, kc_ref, vp_ref, vc_ref, o_ref, *, blocks_per_seq):
    n = pl.program_id(0) % blocks_per_seq
    W = WINDOW
    qi = lax.broadcasted_iota(jnp.int32, (W, 2 * W), 0)
    kj = lax.broadcasted_iota(jnp.int32, (W, 2 * W), 1)
    dist = qi - kj + W
    valid = (dist >= 0) & (dist < W) & ((kj >= W) | (n > 0))
    neg_dist = jnp.where(valid, -dist.astype(F32), NEG_BIG)

    lane = lax.broadcasted_iota(jnp.int32, (2 * W, LANES), 1)
    lo = lane < HEAD_DIM
    lo_w = lax.broadcasted_iota(jnp.int32, (W, LANES), 1) < HEAD_DIM

    def dup_halves(prev_ref, cur_ref, g):
        a = jnp.concatenate([prev_ref[...], cur_ref[...]], axis=0).astype(F32)
        r = pltpu.roll(a, HEAD_DIM, 1)
        return jnp.where(lo, a, r) if g == 0 else jnp.where(lo, r, a)

    for g in range(N_KV_HEADS):
        kd = dup_halves(kp_ref, kc_ref, g)
        vd = dup_halves(vp_ref, vc_ref, g)
        zero = jnp.zeros_like(kd)
        kbd = jnp.concatenate([jnp.where(lo, kd, zero), jnp.where(lo, zero, kd)], axis=0).astype(BF16)
        vbd = jnp.concatenate([jnp.where(lo, vd, zero), jnp.where(lo, zero, vd)], axis=0).astype(BF16)
        for pp in range(GQA_GROUP // 2):
            pair = g * (GQA_GROUP // 2) + pp
            h_e, h_o = 2 * pair, 2 * pair + 1
            q2 = q_ref[:, pair * LANES:(pair + 1) * LANES]
            s = _dot_nt(q2, kbd)
            outs = []
            for half, h in ((0, h_e), (1, h_o)):
                sh = s[:, half * 2 * W:(half + 1) * 2 * W] + _alibi_slope(h) * neg_dist
                sink = sink_ref[h]
                m = jnp.maximum(jnp.max(sh, axis=1, keepdims=True), sink)
                p = jnp.exp(sh - m)
                l = jnp.sum(p, axis=1, keepdims=True) + jnp.exp(sink - m)
                outs.append((p, l))
            p_all = jnp.concatenate([outs[0][0], outs[1][0]], axis=1).astype(BF16)
            o2 = _dot(p_all, vbd)
            inv = jnp.where(lo_w, 1.0 / outs[0][1], 1.0 / outs[1][1])
            o_ref[:, pair * LANES:(pair + 1) * LANES] = (o2 * inv).astype(o_ref.dtype)


def _attention(proj, sinks, B, S):
    T = B * S
    nb = S // WINDOW
    kcol = C_AK // LANES
    vcol = C_AV // LANES
    kern = functools.partial(_attn_kernel, blocks_per_seq=nb)
    grid_spec = pltpu.PrefetchScalarGridSpec(
        num_scalar_prefetch=1,
        grid=(T // WINDOW,),
        in_specs=[
            pl.BlockSpec((WINDOW, ATTN_Q_W), lambda i, s: (i, C_AQ // ATTN_Q_W)),
            pl.BlockSpec((WINDOW, LANES), lambda i, s: (jnp.maximum(i - 1, 0), kcol)),
            pl.BlockSpec((WINDOW, LANES), lambda i, s: (i, kcol)),
            pl.BlockSpec((WINDOW, LANES), lambda i, s: (jnp.maximum(i - 1, 0), vcol)),
            pl.BlockSpec((WINDOW, LANES), lambda i, s: (i, vcol)),
        ],
        out_specs=pl.BlockSpec((WINDOW, ATTN_Q_W), lambda i, s: (i, 0)),
    )
    return pl.pallas_call(
        kern,
        grid_spec=grid_spec,
        out_shape=jax.ShapeDtypeStruct((T, ATTN_Q_W), BF16),
        compiler_params=_cparams(1),
        name="swa_attention",
    )(sinks, proj, proj, proj, proj, proj)


def _soft_cap(z):
    return GATE_SOFTCAP * jnp.tanh(z / GATE_SOFTCAP)


def _mlstm_kernel(q_ref, k_ref, v_ref, og_ref, gates_ref, gbias_ref, normg_ref, o_ref, c_ref, n_ref, m_ref):
    L = M_CHUNK

    @pl.when(pl.program_id(1) == 0)
    def _():
        c_ref[...] = jnp.zeros_like(c_ref)
        n_ref[...] = jnp.zeros_like(n_ref)
        m_ref[...] = jnp.zeros_like(m_ref)

    sc = _soft_cap(gates_ref[...] + gbias_ref[...])
    lf = jax.nn.log_sigmoid(sc)
    ti = lax.broadcasted_iota(jnp.int32, (L, L), 0)
    si = lax.broadcasted_iota(jnp.int32, (L, L), 1)
    causal = si <= ti
    tril = causal.astype(F32)
    bcum = jnp.dot(tril, lf, precision=lax.Precision.HIGHEST, preferred_element_type=F32)
    sc_t = sc.T
    bcum_t = bcum.T

    for h in range(M_HEADS):
        ig_col = sc[:, h:h + 1]
        ig_row = sc_t[h:h + 1, :]
        b_col = bcum[:, M_HEADS + h:M_HEADS + h + 1]
        b_row = bcum_t[M_HEADS + h:M_HEADS + h + 1, :]
        m_prev = m_ref[h][:, :1]
        C = c_ref[h]
        nvec = n_ref[h]

        qx = q_ref[:, h * M_DQK:(h + 1) * M_DQK]
        kx = k_ref[:, h * M_DQK:(h + 1) * M_DQK]
        vx = v_ref[:, h * M_DV:(h + 1) * M_DV]

        inter = b_col + m_prev
        dmat = jnp.where(causal, b_col - b_row + ig_row, NEG_BIG)
        mt = jnp.maximum(inter, jnp.max(dmat, axis=1, keepdims=True))
        w = jnp.exp(dmat - mt)
        sqk = _dot_nt(qx, kx) * w
        s_i = jnp.exp(inter - mt)
        num = s_i * _dot(qx, C.astype(BF16)) + _dot(sqk.astype(BF16), vx)
        nq = s_i * jnp.sum(qx.astype(F32) * nvec, axis=1, keepdims=True) + jnp.sum(sqk, axis=1, keepdims=True)
        hh = num / jnp.maximum(jnp.abs(nq), jnp.exp(-mt))

        bl = b_col[L - 1:L, :]
        g_col = bl - b_col + ig_col
        m_new = jnp.maximum(bl + m_prev, jnp.max(g_col, axis=0, keepdims=True))
        decay = jnp.exp(bl + m_prev - m_new)
        wg = jnp.exp(g_col - m_new)
        kw = kx.astype(F32) * wg
        c_ref[h] = decay * C + _dot(kw.T.astype(BF16), vx)
        n_ref[h] = decay * nvec + jnp.sum(kw, axis=0, keepdims=True)
        m_ref[h] = jnp.broadcast_to(m_new, (1, LANES))

        hn = hh * lax.rsqrt(jnp.mean(hh * hh, axis=1, keepdims=True) + RMS_EPS) * normg_ref[h]
        og = og_ref[:, h * M_DV:(h + 1) * M_DV].astype(F32)
        o_ref[:, h * M_DV:(h + 1) * M_DV] = (hn * jax.nn.sigmoid(og)).astype(o_ref.dtype)


def _mlstm(proj, gates, gbias, norm_g, B, S):
    T = B * S
    nc = S // M_CHUNK
    L = M_CHUNK
    row = lambda b, c: b * nc + c
    return pl.pallas_call(
        _mlstm_kernel,
        grid=(B, nc),
        in_specs=[
            pl.BlockSpec((L, M_QK_W), lambda b, c: (row(b, c), C_MQ // M_QK_W)),
            pl.BlockSpec((L, M_QK_W), lambda b, c: (row(b, c), C_MK // M_QK_W)),
            pl.BlockSpec((L, M_V_W), lambda b, c: (row(b, c), C_MV // M_V_W)),
            pl.BlockSpec((L, M_V_W), lambda b, c: (row(b, c), C_MO // M_V_W)),
            pl.BlockSpec((L, LANES), lambda b, c: (row(b, c), 0)),
            pl.BlockSpec((1, LANES), lambda b, c: (0, 0)),
            pl.BlockSpec((M_HEADS, 1, M_DV), lambda b, c: (0, 0, 0)),
        ],
        out_specs=pl.BlockSpec((L, M_V_W), lambda b, c: (row(b, c), 0)),
        out_shape=jax.ShapeDtypeStruct((T, M_V_W), BF16),
        scratch_shapes=[
            pltpu.VMEM((M_HEADS, M_DQK, M_DV), F32),
            pltpu.VMEM((M_HEADS, 1, M_DQK), F32),
            pltpu.VMEM((M_HEADS, 1, LANES), F32),
        ],
        compiler_params=_cparams(2),
        name="mlstm_chunkwise",
    )(proj, proj, proj, proj, gates, gbias, norm_g)


def _mix_kernel(attn_ref, mh_ref, ga_ref, gb_ref, x_ref, lng_ref, lnb_ref, wba_ref, wbm_ref, wout_ref,
                mg_ref, mb_ref, o_ref, ob_ref):
    a = _dot(attn_ref[...], wba_ref[...])
    m = _dot(mh_ref[...], wbm_ref[...])
    merged = jax.nn.sigmoid(ga_ref[...].astype(F32)) * a + jax.nn.sigmoid(gb_ref[...].astype(F32)) * m
    y = _dot(merged.astype(BF16), wout_ref[...])
    h0 = _layer_norm(x_ref[...], lng_ref[...], lnb_ref[...])
    x1 = _layer_norm(DEEPNORM_ALPHA * h0 + y, mg_ref[...], mb_ref[...])
    o_ref[...] = x1
    ob_ref[...] = _pack_bf16_pair(x1[:, :HALF_D], x1[:, HALF_D:])


def _mix(attn, mh, proj, x2, ln_g, ln_b, w_ba, w_bm, w_out, mix_g, mix_b, tm):
    T, D = x2.shape
    return pl.pallas_call(
        _mix_kernel,
        grid=(T // tm,),
        in_specs=[
            pl.BlockSpec((tm, ATTN_Q_W), lambda i: (i, 0)),
            pl.BlockSpec((tm, M_V_W), lambda i: (i, 0)),
            pl.BlockSpec((tm, D), lambda i: (i, C_GA // D_MODEL)),
            pl.BlockSpec((tm, D), lambda i: (i, C_GB // D_MODEL)),
            pl.BlockSpec((tm, D), lambda i: (i, 0)),
            _const_spec((1, D)), _const_spec((1, D)),
            _const_spec((ATTN_Q_W, D)), _const_spec((M_V_W, D)), _const_spec((D, D)),
            _const_spec((1, D)), _const_spec((1, D)),
        ],
        out_specs=[pl.BlockSpec((tm, D), lambda i: (i, 0)), pl.BlockSpec((tm, HALF_D), lambda i: (i, 0))],
        out_shape=[jax.ShapeDtypeStruct((T, D), F32), jax.ShapeDtypeStruct((T, HALF_D), jnp.uint32)],
        compiler_params=_cparams(1),
        name="mix_outproj_ln",
    )(attn, mh, proj, proj, x2, ln_g, ln_b, w_ba, w_bm, w_out, mix_g, mix_b)


def _router_kernel(x_ref, wa_ref, wb_ref, br_ref, idx_ref, gw_ref, rank_ref, cnt_ref, carry_ref):
    i = pl.program_id(0)
    tm = x_ref.shape[0]
    E = N_EXPERTS

    @pl.when(i == 0)
    def _():
        carry_ref[...] = jnp.zeros_like(carry_ref)

    x = x_ref[...]
    xh = x.astype(BF16)
    xl = (x - xh.astype(F32)).astype(BF16)
    c = _dot(xh, wa_ref[...]) + _dot(xl, wb_ref[...])
    logits_t = (c + pltpu.roll(c, E, 1)).T[:E]
    scores = jax.nn.sigmoid(logits_t)
    sel = scores + br_ref[...]
    erow = lax.broadcasted_iota(jnp.int32, (E, tm), 0).astype(F32)

    idx_rows, g_rows, hits = [], [], []
    for _ in range(TOP_K):
        mx = jnp.max(sel, axis=0, keepdims=True)
        am = jnp.min(jnp.where(sel == mx, erow, float(E)), axis=0, keepdims=True)
        hit = erow == am
        g_rows.append(jnp.sum(jnp.where(hit, scores, 0.0), axis=0, keepdims=True))
        sel = jnp.where(hit, -jnp.inf, sel)
        idx_rows.append(am)
        hits.append(hit)
    gsum = functools.reduce(lambda a, b: a + b, g_rows)
    chosen = functools.reduce(lambda a, b: a + b, [h.astype(F32) for h in hits])

    r_i = lax.broadcasted_iota(jnp.int32, (tm, tm), 0)
    c_i = lax.broadcasted_iota(jnp.int32, (tm, tm), 1)
    before = (r_i < c_i).astype(BF16)
    prefix = _dot(chosen.astype(BF16), before) + carry_ref[:, :1]
    rank_rows = [jnp.sum(jnp.where(h, prefix, 0.0), axis=0, keepdims=True) for h in hits]

    krow = lax.broadcasted_iota(jnp.int32, (TOP_K, tm), 0)

    def stack(rows):
        out = jnp.zeros((TOP_K, tm), F32)
        for k, r in enumerate(rows):
            out = jnp.where(krow == k, r, out)
        return out

    idx_ref[...] = stack(idx_rows).astype(jnp.int32)
    gw_ref[...] = stack(g_rows) / gsum * ROUTED_SCALE
    rank_ref[...] = stack(rank_rows).astype(jnp.int32)
    carry_ref[...] = carry_ref[...] + jnp.sum(chosen, axis=1, keepdims=True)
    cnt_ref[...] = carry_ref[...]


def _router(x1, w_a, w_b, b_col, tm):
    T, D = x1.shape
    tok = lambda i: (0, i)
    return pl.pallas_call(
        _router_kernel,
        grid=(T // tm,),
        in_specs=[
            pl.BlockSpec((tm, D), lambda i: (i, 0)),
            _const_spec((D, 2 * N_EXPERTS)),
            _const_spec((D, 2 * N_EXPERTS)),
            _const_spec((N_EXPERTS, 1)),
        ],
        out_specs=[pl.BlockSpec((TOP_K, tm), tok), pl.BlockSpec((TOP_K, tm), tok), pl.BlockSpec((TOP_K, tm), tok),
                   pl.BlockSpec((N_EXPERTS, LANES), lambda i: (0, 0))],
        out_shape=[jax.ShapeDtypeStruct((TOP_K, T), jnp.int32), jax.ShapeDtypeStruct((TOP_K, T), F32),
                   jax.ShapeDtypeStruct((TOP_K, T), jnp.int32), jax.ShapeDtypeStruct((N_EXPERTS, LANES), F32)],
        scratch_shapes=[pltpu.VMEM((N_EXPERTS, LANES), F32)],
        compiler_params=_cparams(1),
        name="router_topk_rank",
    )(x1, w_a, w_b, b_col)


ZFILL_SIZES = (256, 128, 64, 32, 16, 8)


def _dispatch_kernel(zrow_ref, dest_hbm, xp_ref, xs_hbm, idx_smem, zbuf, isem, zsem, rsem):
    i = pl.program_id(0)
    nsteps = pl.num_programs(0)
    tm = xp_ref.shape[0]
    n = tm * TOP_K

    def idx_copy(tile, slot):
        return pltpu.make_async_copy(dest_hbm.at[pl.ds(tile * n, n)], idx_smem.at[slot], isem.at[slot])

    def zero_copy(row, rows):
        return pltpu.make_async_copy(zbuf.at[pl.ds(0, rows)], xs_hbm.at[pl.ds(pl.multiple_of(row, 8), rows)], zsem)

    def zero_fill_plan():
        plan = []
        for e in range(N_EXPERTS):
            start = zrow_ref[e]
            rem = zrow_ref[N_EXPERTS + e] - start
            for rows in ZFILL_SIZES:
                take = rem >= rows
                plan.append((take, start, rows))
                start = jnp.where(take, start + rows, start)
                rem = jnp.where(take, rem - rows, rem)
        n_alloc = xs_hbm.shape[0]
        for j in range(N_EXPERTS + 2):
            s = zrow_ref[2 * N_EXPERTS] + j * MOE_BLOCK
            plan.append((s < n_alloc, jnp.minimum(s, n_alloc - MOE_BLOCK), MOE_BLOCK))
        return plan

    @pl.when(i == 0)
    def _():
        idx_copy(0, 0).start()
        zbuf[...] = jnp.zeros_like(zbuf)
        plan = zero_fill_plan()
        for take, start, rows in plan:
            @pl.when(take)
            def _():
                zero_copy(start, rows).start()
        for take, start, rows in plan:
            @pl.when(take)
            def _():
                zero_copy(0, rows).wait()

    slot = i % 2
    idx_copy(i, slot).wait()

    @pl.when(i + 1 < nsteps)
    def _():
        idx_copy(i + 1, 1 - slot).start()

    def row_copy(t, d):
        return pltpu.make_async_copy(xp_ref.at[pl.ds(t, 1)], xs_hbm.at[pl.ds(d, 1)], rsem)

    def body(t, carry):
        for k in range(TOP_K):
            row_copy(t, idx_smem[slot, t * TOP_K + k]).start()
        return carry

    lax.fori_loop(0, tm, body, 0)
    for _ in range(TOP_K):
        pltpu.make_async_copy(xp_ref, xs_hbm.at[pl.ds(0, tm)], rsem).wait()


def _dispatch(x1p, dest_flat, zrow, n_rows_alloc, tm):
    T, W = x1p.shape
    grid_spec = pltpu.PrefetchScalarGridSpec(
        num_scalar_prefetch=1,
        grid=(T // tm,),
        in_specs=[
            pl.BlockSpec(memory_space=pl.ANY),
            pl.BlockSpec((tm, W), lambda i, z: (i, 0)),
        ],
        out_specs=pl.BlockSpec(memory_space=pl.ANY),
        scratch_shapes=[
            pltpu.SMEM((2, tm * TOP_K), jnp.int32),
            pltpu.VMEM((MOE_BLOCK, W), jnp.uint32),
            pltpu.SemaphoreType.DMA((2,)),
            pltpu.SemaphoreType.DMA(()),
            pltpu.SemaphoreType.DMA(()),
        ],
    )
    return pl.pallas_call(
        _dispatch_kernel,
        grid_spec=grid_spec,
        out_shape=jax.ShapeDtypeStruct((n_rows_alloc, W), jnp.uint32),
        compiler_params=_cparams(1),
        name="moe_dispatch",
    )(zrow, dest_flat, x1p)


def _expert_kernel(be_ref, nused_ref, xs_ref, wg_ref, wu_ref, wd_ref, y_ref):
    i = pl.program_id(0)

    @pl.when(i < nused_ref[0])
    def _():
        lo, hi = _unpack_bf16_pair(xs_ref[...])
        xb = jnp.concatenate([lo.astype(BF16), hi.astype(BF16)], axis=1)
        hb = jax.nn.silu(_dot(xb, wg_ref[0])) * _dot(xb, wu_ref[0])
        y = _dot(hb.astype(BF16), wd_ref[0])
        y_ref[...] = _pack_bf16_pair(y[:, :HALF_D], y[:, HALF_D:])

    @pl.when(i >= nused_ref[0])
    def _():
        y_ref[...] = jnp.zeros_like(y_ref)


def _experts(xs, block_e, n_used, w_eg, w_eu, w_ed, n_blocks):
    W = xs.shape[1]
    D = 2 * W
    last_used = lambda i, nu: jnp.minimum(i, nu[0] - 1)
    grid_spec = pltpu.PrefetchScalarGridSpec(
        num_scalar_prefetch=2,
        grid=(n_blocks,),
        in_specs=[
            pl.BlockSpec((MOE_BLOCK, W), lambda i, be, nu: (last_used(i, nu), 0)),
            pl.BlockSpec((1, D, D_EXPERT), lambda i, be, nu: (be[i], 0, 0)),
            pl.BlockSpec((1, D, D_EXPERT), lambda i, be, nu: (be[i], 0, 0)),
            pl.BlockSpec((1, D_EXPERT, D), lambda i, be, nu: (be[i], 0, 0)),
        ],
        out_specs=pl.BlockSpec((MOE_BLOCK, W), lambda i, be, nu: (i, 0)),
    )
    return pl.pallas_call(
        _expert_kernel,
        grid_spec=grid_spec,
        out_shape=jax.ShapeDtypeStruct((n_blocks * MOE_BLOCK, W), jnp.uint32),
        compiler_params=_cparams(1),
        name="expert_mlp",
    )(block_e, n_used, xs, w_eg, w_eu, w_ed)


def _ffn_ple_kernel(dest_hbm, y_hbm, x1_ref, gw_ref, p_ref, wsg_ref, wsu_ref, wsd_ref, fg_ref, fb_ref,
                    wpp_ref, wpg_ref, pg_ref, pb_ref, o_ref, idx_smem, ybuf, isem, rsem):
    i = pl.program_id(0)
    nsteps = pl.num_programs(0)
    tm = x1_ref.shape[0]
    n = tm * TOP_K

    def idx_copy(tile, slot):
        return pltpu.make_async_copy(dest_hbm.at[pl.ds(tile * n, n)], idx_smem.at[slot], isem.at[slot])

    @pl.when(i == 0)
    def _():
        idx_copy(0, 0).start()

    slot = i % 2
    idx_copy(i, slot).wait()

    @pl.when(i + 1 < nsteps)
    def _():
        idx_copy(i + 1, 1 - slot).start()

    def body(t, carry):
        for k in range(TOP_K):
            d = idx_smem[slot, t * TOP_K + k]
            pltpu.make_async_copy(y_hbm.at[pl.ds(d, 1)], ybuf.at[k, pl.ds(t, 1)], rsem).start()
        return carry

    lax.fori_loop(0, tm, body, 0)

    x1 = x1_ref[...]
    xb = x1.astype(BF16)
    hs = jax.nn.silu(_dot(xb, wsg_ref[...])) * _dot(xb, wsu_ref[...])
    shared = _dot(hs.astype(BF16), wsd_ref[...])

    for k in range(TOP_K):
        pltpu.make_async_copy(y_hbm.at[pl.ds(0, tm)], ybuf.at[k], rsem).wait()
    gw = gw_ref[...]
    acc_lo = jnp.zeros((tm, HALF_D), F32)
    acc_hi = jnp.zeros((tm, HALF_D), F32)
    for k in range(TOP_K):
        lo, hi = _unpack_bf16_pair(ybuf[k])
        wk = gw[:, k:k + 1]
        acc_lo = acc_lo + wk * lo
        acc_hi = acc_hi + wk * hi
    routed = jnp.concatenate([acc_lo, acc_hi], axis=1)

    x2 = _layer_norm(DEEPNORM_ALPHA * x1 + (routed + shared), fg_ref[...], fb_ref[...])
    ple = _dot(p_ref[...].astype(BF16), wpp_ref[...]) * jax.nn.sigmoid(_dot(x2.astype(BF16), wpg_ref[...]))
    o_ref[...] = _layer_norm(DEEPNORM_ALPHA * x2 + ple, pg_ref[...], pb_ref[...])


def _ffn_ple(dest_flat, y, x1, gw_tok, p2, w_sg, w_su, w_sd, ffn_g, ffn_b, w_pp, w_pg, ple_g, ple_b, tm):
    T, D = x1.shape
    return pl.pallas_call(
        _ffn_ple_kernel,
        grid=(T // tm,),
        in_specs=[
            pl.BlockSpec(memory_space=pl.ANY),
            pl.BlockSpec(memory_space=pl.ANY),
            pl.BlockSpec((tm, D), lambda i: (i, 0)),
            pl.BlockSpec((tm, TOP_K), lambda i: (i, 0)),
            pl.BlockSpec((tm, PLE_DIM), lambda i: (i, 0)),
            _const_spec((D, D_SHARED)), _const_spec((D, D_SHARED)), _const_spec((D_SHARED, D)),
            _const_spec((1, D)), _const_spec((1, D)),
            _const_spec((PLE_DIM, D)), _const_spec((D, D)),
            _const_spec((1, D)), _const_spec((1, D)),
        ],
        out_specs=pl.BlockSpec((tm, D), lambda i: (i, 0)),
        out_shape=jax.ShapeDtypeStruct((T, D), F32),
        scratch_shapes=[
            pltpu.SMEM((2, tm * TOP_K), jnp.int32),
            pltpu.VMEM((TOP_K, tm, HALF_D), jnp.uint32),
            pltpu.SemaphoreType.DMA((2,)),
            pltpu.SemaphoreType.DMA(()),
        ],
        compiler_params=_cparams(1),
        name="combine_ffn_ple_ln",
    )(dest_flat, y, x1, gw_tok, p2, w_sg, w_su, w_sd, ffn_g, ffn_b, w_pp, w_pg, ple_g, ple_b)


def _permute_w_in(w_in):
    D = w_in.shape[0]
    sizes = (ATTN_Q_W, ATTN_KV_W, ATTN_KV_W, M_QK_W, M_QK_W, M_V_W, M_V_W, M_HEADS, M_HEADS, D_MODEL, D_MODEL)
    offs = np.concatenate([[0], np.cumsum(sizes)])
    seg = [w_in[:, offs[k]:offs[k + 1]] for k in range(len(sizes))]
    aq, ak, av, mq, mk, mv, mo, mi, mf, ga, gb = seg
    aq = aq * (HEAD_DIM ** -0.5)
    mk = mk * (M_DQK ** -0.5)
    gates = jnp.concatenate([mi, mf, jnp.zeros((D, LANES - 2 * M_HEADS), w_in.dtype)], axis=1)
    pad = jnp.zeros((D, PROJ_COLS - (C_IF + LANES)), w_in.dtype)
    return jnp.concatenate([ga, gb, aq, mv, mo, mq, mk, ak, av, gates, pad], axis=1).astype(BF16)


def _route_layout(idx_t, rank_t, counts, T):
    n_pairs = T * TOP_K
    n_blocks = -(-n_pairs // MOE_BLOCK) + N_EXPERTS
    padded = (counts + MOE_BLOCK - 1) // MOE_BLOCK * MOE_BLOCK
    ends = jnp.cumsum(padded)
    pstarts = ends - padded
    onehot = idx_t[:, :, None] == jnp.arange(N_EXPERTS, dtype=jnp.int32)
    dest_t = jnp.sum(jnp.where(onehot, pstarts.astype(jnp.int32), 0), axis=-1) + rank_t
    dest_flat = dest_t.T.reshape(-1)
    block_e = jnp.minimum(jnp.searchsorted(ends, jnp.arange(n_blocks) * MOE_BLOCK, side='right'),
                          N_EXPERTS - 1).astype(jnp.int32)
    n_used = (ends[-1] // MOE_BLOCK).astype(jnp.int32).reshape(1)
    zrow = jnp.concatenate([(pstarts + counts) // 8 * 8, ends, ends[-1:]]).astype(jnp.int32)
    return dest_flat, block_e, n_used, zrow, n_blocks


def kernel(x, p, ln_in_g, ln_in_b, w_in, attn_sinks, mlstm_b_i, mlstm_b_f, mlstm_norm_g, w_branch_attn, w_branch_mlstm, w_out, ln_mix_g, ln_mix_b, w_router, b_router, w_exp_gate, w_exp_up, w_exp_down, w_sh_gate, w_sh_up, w_sh_down, ln_ffn_g, ln_ffn_b, w_ple_proj, w_ple_gate, ln_ple_g, ln_ple_b):
    B, S, D = x.shape
    T = B * S
    assert D == D_MODEL and S % WINDOW == 0 and S % M_CHUNK == 0 and w_in.shape[0] == DEPTH
    x2 = x.reshape(T, D)
    row = lambda v: v.reshape(1, -1).astype(F32)

    tm_proj = min(1024, T)
    tm_mix = min(256, T)
    tm_router = min(512, T)
    tm_dispatch = min(512, T)

    proj, gates = _in_projection(x2, row(ln_in_g), row(ln_in_b), _permute_w_in(w_in[0]), tm_proj)

    attn = _attention(proj, attn_sinks[0].astype(F32), B, S)

    gbias = jnp.concatenate([mlstm_b_i[0], mlstm_b_f[0], jnp.zeros((LANES - 2 * M_HEADS,), F32)]).reshape(1, LANES)
    mh = _mlstm(proj, gates, gbias, mlstm_norm_g[0].reshape(M_HEADS, 1, M_DV).astype(F32), B, S)

    x1, x1p = _mix(attn, mh, proj, x2, row(ln_in_g), row(ln_in_b),
                   w_branch_attn[0].astype(BF16), w_branch_mlstm[0].astype(BF16), w_out[0].astype(BF16),
                   row(ln_mix_g[0]), row(ln_mix_b[0]), tm_mix)

    wr = w_router[0].astype(F32)
    wr_hi = wr.astype(BF16)
    wr_lo = (wr - wr_hi.astype(F32)).astype(BF16)
    w_a = jnp.concatenate([wr_hi, wr_lo], axis=1)
    w_b = jnp.concatenate([jnp.zeros_like(wr_hi), wr_hi], axis=1)
    idx_t, gw_t, rank_t, counts_f = _router(x1, w_a, w_b, b_router[0].reshape(N_EXPERTS, 1).astype(F32), tm_router)

    counts = counts_f[:, 0].astype(jnp.int32)
    dest_flat, block_e, n_used, zrow, n_blocks = _route_layout(idx_t, rank_t, counts, T)
    xs = _dispatch(x1p, dest_flat, zrow, n_blocks * MOE_BLOCK + 2 * MOE_BLOCK, tm_dispatch)
    y = _experts(xs, block_e, n_used,
                 w_exp_gate[0].astype(BF16), w_exp_up[0].astype(BF16), w_exp_down[0].astype(BF16), n_blocks)

    out = _ffn_ple(dest_flat, y, x1, gw_t.T, p[0].reshape(T, PLE_DIM),
                   w_sh_gate[0].astype(BF16), w_sh_up[0].astype(BF16), w_sh_down[0].astype(BF16),
                   row(ln_ffn_g[0]), row(ln_ffn_b[0]),
                   w_ple_proj[0].astype(BF16), w_ple_gate[0].astype(BF16),
                   row(ln_ple_g[0]), row(ln_ple_b[0]), tm_mix)
    return out.reshape(B, S, D)
```

```python
import functools

import numpy as np
import jax
import jax.numpy as jnp
from jax import lax
from jax.experimental import pallas as pl
from jax.experimental.pallas import tpu as pltpu

F32 = jnp.float32
BF16 = jnp.bfloat16

D_MODEL = 2048
PLE_DIM = 256
HEAD_DIM = 64
N_HEADS = 16
N_KV_HEADS = 2
GQA_GROUP = N_HEADS // N_KV_HEADS
WINDOW = 128
M_HEADS = 4
M_DV = 256
M_DQK = 128
M_CHUNK = 64
GATE_SOFTCAP = 15.0
N_EXPERTS = 64
TOP_K = 8
D_EXPERT = 512
D_SHARED = 512
ROUTED_SCALE = 2.5
MOE_BLOCK = 256
LN_EPS = 1e-5
RMS_EPS = 1e-6
DEPTH = 1
DEEPNORM_ALPHA = (2.0 * DEPTH) ** 0.25

ATTN_Q_W = N_HEADS * HEAD_DIM
ATTN_KV_W = N_KV_HEADS * HEAD_DIM
M_QK_W = M_HEADS * M_DQK
M_V_W = M_HEADS * M_DV

LANES = 128
NEG_BIG = -1e30

C_GA = 0
C_GB = C_GA + D_MODEL
C_AQ = C_GB + D_MODEL
C_MV = C_AQ + ATTN_Q_W
C_MO = C_MV + M_V_W
C_MQ = C_MO + M_V_W
C_MK = C_MQ + M_QK_W
C_AK = C_MK + M_QK_W
C_AV = C_AK + ATTN_KV_W
C_IF = C_AV + ATTN_KV_W
PROJ_TN = 512
PROJ_COLS = 8704
IF_TILE = C_IF // PROJ_TN
IF_OFF = C_IF - IF_TILE * PROJ_TN

SUBLANES = 8
ROW_TILES = D_MODEL // LANES
assert ROW_TILES % SUBLANES == 0

VMEM_LIMIT = 52 * 1024 * 1024


def _cparams(n_axes):
    return pltpu.CompilerParams(dimension_semantics=("arbitrary",) * n_axes, vmem_limit_bytes=VMEM_LIMIT)


def _const_spec(shape):
    nd = len(shape)
    return pl.BlockSpec(shape, lambda *_: (0,) * nd, pipeline_mode=pl.Buffered(1))


def _layer_norm(x, g, b):
    mu = jnp.mean(x, axis=-1, keepdims=True)
    xc = x - mu
    var = jnp.mean(xc * xc, axis=-1, keepdims=True)
    return xc * lax.rsqrt(var + LN_EPS) * g + b


def _dot(a, b):
    return jnp.dot(a, b, preferred_element_type=F32)


def _dot_nt(a, b):
    return lax.dot_general(a, b, (((1,), (1,)), ((), ())), preferred_element_type=F32)


def _store_rows(ref, val):
    rows = val.shape[0]
    for c in range(ROW_TILES):
        ref[pl.ds(c, rows, stride=ROW_TILES), :] = val[:, c * LANES:(c + 1) * LANES]


def _load_rows(ref):
    rows = ref.shape[0] // ROW_TILES
    return jnp.concatenate([ref[pl.ds(c, rows, stride=ROW_TILES), :] for c in range(ROW_TILES)], axis=1)


def _tile_rows(r):
    return pl.ds(pl.multiple_of(r * ROW_TILES, ROW_TILES), ROW_TILES)


def _inproj_kernel(x_ref, g_ref, b_ref, w_ref, o_ref, og_ref, xn_ref):
    j = pl.program_id(1)

    @pl.when(j == 0)
    def _():
        xn_ref[...] = _layer_norm(x_ref[...], g_ref[...], b_ref[...]).astype(BF16)

    acc = _dot(xn_ref[...], w_ref[...])
    o_ref[...] = acc.astype(o_ref.dtype)

    @pl.when(j == IF_TILE)
    def _():
        og_ref[...] = acc[:, IF_OFF:IF_OFF + LANES]


def _in_projection(x2, ln_g, ln_b, w_perm, tm):
    T, D = x2.shape
    return pl.pallas_call(
        _inproj_kernel,
        grid=(T // tm, PROJ_COLS // PROJ_TN),
        in_specs=[
            pl.BlockSpec((tm, D), lambda i, j: (i, 0)),
            pl.BlockSpec((1, D), lambda i, j: (0, 0)),
            pl.BlockSpec((1, D), lambda i, j: (0, 0)),
            pl.BlockSpec((D, PROJ_TN), lambda i, j: (0, j)),
        ],
        out_specs=[
            pl.BlockSpec((tm, PROJ_TN), lambda i, j: (i, j)),
            pl.BlockSpec((tm, LANES), lambda i, j: (i, 0)),
        ],
        out_shape=[
            jax.ShapeDtypeStruct((T, PROJ_COLS), BF16),
            jax.ShapeDtypeStruct((T, LANES), F32),
        ],
        scratch_shapes=[pltpu.VMEM((tm, D), BF16)],
        compiler_params=_cparams(2),
        name="ln_inproj",
    )(x2, ln_g, ln_b, w_perm)


def _alibi_slope(h):
    return float(2.0 ** (-8.0 / N_HEADS * (h + 1)))


def _attn_kernel(sink_ref, q_ref, kp_ref, kc_ref, vp_ref, vc_ref, o_ref, *, blocks_per_seq):
    n = pl.program_id(0) % blocks_per_seq
    W = WINDOW
    qi = lax.broadcasted_iota(jnp.int32, (W, 2 * W), 0)
    kj = lax.broadcasted_iota(jnp.int32, (W, 2 * W), 1)
    dist = qi - kj + W
    valid = (dist >= 0) & (dist < W) & ((kj >= W) | (n > 0))
    neg_dist = jnp.where(valid, -dist.astype(F32), NEG_BIG)

    lane = lax.broadcasted_iota(jnp.int32, (2 * W, LANES), 1)
    lo = lane < HEAD_DIM
    lo_w = lax.broadcasted_iota(jnp.int32, (W, LANES), 1) < HEAD_DIM

    def dup_halves(prev_ref, cur_ref, g):
        a = jnp.concatenate([prev_ref[...], cur_ref[...]], axis=0).astype(F32)
        r = pltpu.roll(a, HEAD_DIM, 1)
        return jnp.where(lo, a, r) if g == 0 else jnp.where(lo, r, a)

    for g in range(N_KV_HEADS):
        kd = dup_halves(kp_ref, kc_ref, g)
        vd = dup_halves(vp_ref, vc_ref, g)
        zero = jnp.zeros_like(kd)
        kbd = jnp.concatenate([jnp.where(lo, kd, zero), jnp.where(lo, zero, kd)], axis=0).astype(BF16)
        vbd = jnp.concatenate([jnp.where(lo, vd, zero), jnp.where(lo, zero, vd)], axis=0).astype(BF16)
        for pp in range(GQA_GROUP // 2):
            pair = g * (GQA_GROUP // 2) + pp
            h_e, h_o = 2 * pair, 2 * pair + 1
            q2 = q_ref[:, pair * LANES:(pair + 1) * LANES]
            s = _dot_nt(q2, kbd)
            outs = []
            for half, h in ((0, h_e), (1, h_o)):
                sh = s[:, half * 2 * W:(half + 1) * 2 * W] + _alibi_slope(h) * neg_dist
                sink = sink_ref[h]
                m = jnp.maximum(jnp.max(sh, axis=1, keepdims=True), sink)
                p = jnp.exp(sh - m)
                l = jnp.sum(p, axis=1, keepdims=True) + jnp.exp(sink - m)
                outs.append((p, l))
            p_all = jnp.concatenate([outs[0][0], outs[1][0]], axis=1).astype(BF16)
            o2 = _dot(p_all, vbd)
            inv = jnp.where(lo_w, 1.0 / outs[0][1], 1.0 / outs[1][1])
            o_ref[:, pair * LANES:(pair + 1) * LANES] = (o2 * inv).astype(o_ref.dtype)


def _attention(proj, sinks, B, S):
    T = B * S
    nb = S // WINDOW
    kcol = C_AK // LANES
    vcol = C_AV // LANES
    kern = functools.partial(_attn_kernel, blocks_per_seq=nb)
    grid_spec = pltpu.PrefetchScalarGridSpec(
        num_scalar_prefetch=1,
        grid=(T // WINDOW,),
        in_specs=[
            pl.BlockSpec((WINDOW, ATTN_Q_W), lambda i, s: (i, C_AQ // ATTN_Q_W)),
            pl.BlockSpec((WINDOW, LANES), lambda i, s: (jnp.maximum(i - 1, 0), kcol)),
            pl.BlockSpec((WINDOW, LANES), lambda i, s: (i, kcol)),
            pl.BlockSpec((WINDOW, LANES), lambda i, s: (jnp.maximum(i - 1, 0), vcol)),
            pl.BlockSpec((WINDOW, LANES), lambda i, s: (i, vcol)),
        ],
        out_specs=pl.BlockSpec((WINDOW, ATTN_Q_W), lambda i, s: (i, 0)),
    )
    return pl.pallas_call(
        kern,
        grid_spec=grid_spec,
        out_shape=jax.ShapeDtypeStruct((T, ATTN_Q_W), BF16),
        compiler_params=_cparams(1),
        name="swa_attention",
    )(sinks, proj, proj, proj, proj, proj)


def _soft_cap(z):
    return GATE_SOFTCAP * jnp.tanh(z / GATE_SOFTCAP)


def _mlstm_kernel(q_ref, k_ref, v_ref, og_ref, gates_ref, gbias_ref, normg_ref, o_ref, c_ref, n_ref, m_ref):
    L = M_CHUNK

    @pl.when(pl.program_id(1) == 0)
    def _():
        c_ref[...] = jnp.zeros_like(c_ref)
        n_ref[...] = jnp.zeros_like(n_ref)
        m_ref[...] = jnp.zeros_like(m_ref)

    sc = _soft_cap(gates_ref[...] + gbias_ref[...])
    lf = jax.nn.log_sigmoid(sc)
    ti = lax.broadcasted_iota(jnp.int32, (L, L), 0)
    si = lax.broadcasted_iota(jnp.int32, (L, L), 1)
    causal = si <= ti
    tril = causal.astype(F32)
    bcum = jnp.dot(tril, lf, precision=lax.Precision.HIGHEST, preferred_element_type=F32)
    sc_t = sc.T
    bcum_t = bcum.T

    for h in range(M_HEADS):
        ig_col = sc[:, h:h + 1]
        ig_row = sc_t[h:h + 1, :]
        b_col = bcum[:, M_HEADS + h:M_HEADS + h + 1]
        b_row = bcum_t[M_HEADS + h:M_HEADS + h + 1, :]
        m_prev = m_ref[h][:, :1]
        C = c_ref[h]
        nvec = n_ref[h]

        qx = q_ref[:, h * M_DQK:(h + 1) * M_DQK]
        kx = k_ref[:, h * M_DQK:(h + 1) * M_DQK]
        vx = v_ref[:, h * M_DV:(h + 1) * M_DV]

        inter = b_col + m_prev
        dmat = jnp.where(causal, b_col - b_row + ig_row, NEG_BIG)
        mt = jnp.maximum(inter, jnp.max(dmat, axis=1, keepdims=True))
        w = jnp.exp(dmat - mt)
        sqk = _dot_nt(qx, kx) * w
        s_i = jnp.exp(inter - mt)
        num = s_i * _dot(qx, C.astype(BF16)) + _dot(sqk.astype(BF16), vx)
        nq = s_i * jnp.sum(qx.astype(F32) * nvec, axis=1, keepdims=True) + jnp.sum(sqk, axis=1, keepdims=True)
        hh = num / jnp.maximum(jnp.abs(nq), jnp.exp(-mt))

        bl = b_col[L - 1:L, :]
        g_col = bl - b_col + ig_col
        m_new = jnp.maximum(bl + m_prev, jnp.max(g_col, axis=0, keepdims=True))
        decay = jnp.exp(bl + m_prev - m_new)
        wg = jnp.exp(g_col - m_new)
        kw = kx.astype(F32) * wg
        c_ref[h] = decay * C + _dot(kw.T.astype(BF16), vx)
        n_ref[h] = decay * nvec + jnp.sum(kw, axis=0, keepdims=True)
        m_ref[h] = jnp.broadcast_to(m_new, (1, LANES))

        hn = hh * lax.rsqrt(jnp.mean(hh * hh, axis=1, keepdims=True) + RMS_EPS) * normg_ref[h]
        og = og_ref[:, h * M_DV:(h + 1) * M_DV].astype(F32)
        o_ref[:, h * M_DV:(h + 1) * M_DV] = (hn * jax.nn.sigmoid(og)).astype(o_ref.dtype)


def _mlstm(proj, gates, gbias, norm_g, B, S):
    T = B * S
    nc = S // M_CHUNK
    L = M_CHUNK
    row = lambda b, c: b * nc + c
    return pl.pallas_call(
        _mlstm_kernel,
        grid=(B, nc),
        in_specs=[
            pl.BlockSpec((L, M_QK_W), lambda b, c: (row(b, c), C_MQ // M_QK_W)),
            pl.BlockSpec((L, M_QK_W), lambda b, c: (row(b, c), C_MK // M_QK_W)),
            pl.BlockSpec((L, M_V_W), lambda b, c: (row(b, c), C_MV // M_V_W)),
            pl.BlockSpec((L, M_V_W), lambda b, c: (row(b, c), C_MO // M_V_W)),
            pl.BlockSpec((L, LANES), lambda b, c: (row(b, c), 0)),
            pl.BlockSpec((1, LANES), lambda b, c: (0, 0)),
            pl.BlockSpec((M_HEADS, 1, M_DV), lambda b, c: (0, 0, 0)),
        ],
        out_specs=pl.BlockSpec((L, M_V_W), lambda b, c: (row(b, c), 0)),
        out_shape=jax.ShapeDtypeStruct((T, M_V_W), BF16),
        scratch_shapes=[
            pltpu.VMEM((M_HEADS, M_DQK, M_DV), F32),
            pltpu.VMEM((M_HEADS, 1, M_DQK), F32),
            pltpu.VMEM((M_HEADS, 1, LANES), F32),
        ],
        compiler_params=_cparams(2),
        name="mlstm_chunkwise",
    )(proj, proj, proj, proj, gates, gbias, norm_g)


def _mix_kernel(attn_ref, mh_ref, ga_ref, gb_ref, x_ref, lng_ref, lnb_ref, wba_ref, wbm_ref, wout_ref,
                mg_ref, mb_ref, o_ref):
    a = _dot(attn_ref[...], wba_ref[...])
    m = _dot(mh_ref[...], wbm_ref[...])
    merged = jax.nn.sigmoid(ga_ref[...].astype(F32)) * a + jax.nn.sigmoid(gb_ref[...].astype(F32)) * m
    y = _dot(merged.astype(BF16), wout_ref[...])
    h0 = _layer_norm(x_ref[...], lng_ref[...], lnb_ref[...])
    x1 = _layer_norm(DEEPNORM_ALPHA * h0 + y, mg_ref[...], mb_ref[...])
    _store_rows(o_ref, x1)


def _mix(attn, mh, proj, x2, ln_g, ln_b, w_ba, w_bm, w_out, mix_g, mix_b, tm):
    T, D = x2.shape
    return pl.pallas_call(
        _mix_kernel,
        grid=(T // tm,),
        in_specs=[
            pl.BlockSpec((tm, ATTN_Q_W), lambda i: (i, 0)),
            pl.BlockSpec((tm, M_V_W), lambda i: (i, 0)),
            pl.BlockSpec((tm, D), lambda i: (i, C_GA // D_MODEL)),
            pl.BlockSpec((tm, D), lambda i: (i, C_GB // D_MODEL)),
            pl.BlockSpec((tm, D), lambda i: (i, 0)),
            _const_spec((1, D)), _const_spec((1, D)),
            _const_spec((ATTN_Q_W, D)), _const_spec((M_V_W, D)), _const_spec((D, D)),
            _const_spec((1, D)), _const_spec((1, D)),
        ],
        out_specs=pl.BlockSpec((tm * ROW_TILES, LANES), lambda i: (i, 0)),
        out_shape=jax.ShapeDtypeStruct((T * ROW_TILES, LANES), F32),
        compiler_params=_cparams(1),
        name="mix_outproj_ln",
    )(attn, mh, proj, proj, x2, ln_g, ln_b, w_ba, w_bm, w_out, mix_g, mix_b)


def _router_kernel(x_ref, wa_ref, wb_ref, br_ref, idx_ref, gw_ref, rank_ref, cnt_ref, carry_ref):
    i = pl.program_id(0)
    tm = x_ref.shape[0] // ROW_TILES
    E = N_EXPERTS

    @pl.when(i == 0)
    def _():
        carry_ref[...] = jnp.zeros_like(carry_ref)

    x = _load_rows(x_ref)
    xh = x.astype(BF16)
    xl = (x - xh.astype(F32)).astype(BF16)
    c = _dot(xh, wa_ref[...]) + _dot(xl, wb_ref[...])
    logits_t = (c + pltpu.roll(c, E, 1)).T[:E]
    scores = jax.nn.sigmoid(logits_t)
    sel = scores + br_ref[...]
    erow = lax.broadcasted_iota(jnp.int32, (E, tm), 0).astype(F32)

    idx_rows, g_rows, hits = [], [], []
    for _ in range(TOP_K):
        mx = jnp.max(sel, axis=0, keepdims=True)
        am = jnp.min(jnp.where(sel == mx, erow, float(E)), axis=0, keepdims=True)
        hit = erow == am
        g_rows.append(jnp.sum(jnp.where(hit, scores, 0.0), axis=0, keepdims=True))
        sel = jnp.where(hit, -jnp.inf, sel)
        idx_rows.append(am)
        hits.append(hit)
    gsum = functools.reduce(lambda a, b: a + b, g_rows)
    chosen = functools.reduce(lambda a, b: a + b, [h.astype(F32) for h in hits])

    r_i = lax.broadcasted_iota(jnp.int32, (tm, tm), 0)
    c_i = lax.broadcasted_iota(jnp.int32, (tm, tm), 1)
    before = (r_i < c_i).astype(BF16)
    prefix = _dot(chosen.astype(BF16), before) + carry_ref[:, :1]
    rank_rows = [jnp.sum(jnp.where(h, prefix, 0.0), axis=0, keepdims=True) for h in hits]

    krow = lax.broadcasted_iota(jnp.int32, (TOP_K, tm), 0)

    def stack(rows):
        out = jnp.zeros((TOP_K, tm), F32)
        for k, r in enumerate(rows):
            out = jnp.where(krow == k, r, out)
        return out

    idx_ref[...] = stack(idx_rows).astype(jnp.int32)
    gw_ref[...] = stack(g_rows) / gsum * ROUTED_SCALE
    rank_ref[...] = stack(rank_rows).astype(jnp.int32)
    carry_ref[...] = carry_ref[...] + jnp.sum(chosen, axis=1, keepdims=True)
    cnt_ref[...] = carry_ref[...]


def _router(x1, w_a, w_b, b_col, tm):
    T, D = x1.shape[0] // ROW_TILES, D_MODEL
    tok = lambda i: (0, i)
    return pl.pallas_call(
        _router_kernel,
        grid=(T // tm,),
        in_specs=[
            pl.BlockSpec((tm * ROW_TILES, LANES), lambda i: (i, 0)),
            _const_spec((D, 2 * N_EXPERTS)),
            _const_spec((D, 2 * N_EXPERTS)),
            _const_spec((N_EXPERTS, 1)),
        ],
        out_specs=[pl.BlockSpec((TOP_K, tm), tok), pl.BlockSpec((TOP_K, tm), tok), pl.BlockSpec((TOP_K, tm), tok),
                   pl.BlockSpec((N_EXPERTS, LANES), lambda i: (0, 0))],
        out_shape=[jax.ShapeDtypeStruct((TOP_K, T), jnp.int32), jax.ShapeDtypeStruct((TOP_K, T), F32),
                   jax.ShapeDtypeStruct((TOP_K, T), jnp.int32), jax.ShapeDtypeStruct((N_EXPERTS, LANES), F32)],
        scratch_shapes=[pltpu.VMEM((N_EXPERTS, LANES), F32)],
        compiler_params=_cparams(1),
        name="router_topk_rank",
    )(x1, w_a, w_b, b_col)


ZFILL_SIZES = tuple(2 ** b for b in reversed(range(MOE_BLOCK.bit_length() - 1)))
assert MOE_BLOCK & (MOE_BLOCK - 1) == 0


def _dispatch_kernel(zrow_ref, dest_hbm, xp_ref, xs_hbm, idx_smem, zbuf, isem, zsem, rsem):
    i = pl.program_id(0)
    nsteps = pl.num_programs(0)
    tm = xp_ref.shape[0] // ROW_TILES
    n = tm * TOP_K

    def idx_copy(tile, slot):
        return pltpu.make_async_copy(dest_hbm.at[pl.ds(tile * n, n)],
                                     idx_smem.at[pl.ds(pl.multiple_of(slot * n, n), n)], isem.at[slot])

    def zero_copy(row, rows):
        return pltpu.make_async_copy(zbuf.at[pl.ds(0, rows * ROW_TILES)],
                                     xs_hbm.at[pl.ds(pl.multiple_of(row * ROW_TILES, ROW_TILES), rows * ROW_TILES)], zsem)

    def zero_fill_plan():
        plan = []
        for e in range(N_EXPERTS):
            start = zrow_ref[e]
            rem = zrow_ref[N_EXPERTS + e] - start
            for rows in ZFILL_SIZES:
                take = rem >= rows
                plan.append((take, start, rows))
                start = jnp.where(take, start + rows, start)
                rem = jnp.where(take, rem - rows, rem)
        n_alloc = xs_hbm.shape[0] // ROW_TILES
        for j in range(N_EXPERTS + 2):
            s = zrow_ref[2 * N_EXPERTS] + j * MOE_BLOCK
            plan.append((s < n_alloc, jnp.minimum(s, n_alloc - MOE_BLOCK), MOE_BLOCK))
        return plan

    @pl.when(i == 0)
    def _():
        idx_copy(0, 0).start()
        zbuf[...] = jnp.zeros_like(zbuf)
        plan = zero_fill_plan()
        for take, start, rows in plan:
            @pl.when(take)
            def _():
                zero_copy(start, rows).start()
        for take, start, rows in plan:
            @pl.when(take)
            def _():
                zero_copy(0, rows).wait()

    slot = i % 2
    idx_copy(i, slot).wait()

    @pl.when(i + 1 < nsteps)
    def _():
        idx_copy(i + 1, 1 - slot).start()

    base = slot * n

    def body(t, carry):
        for k in range(TOP_K):
            d = idx_smem[base + t * TOP_K + k]
            pltpu.make_async_copy(xp_ref.at[_tile_rows(t)], xs_hbm.at[_tile_rows(d)], rsem).start(priority=k % 2)
        return carry

    lax.fori_loop(0, tm, body, 0)
    for _ in range(TOP_K):
        pltpu.make_async_copy(xp_ref, xs_hbm.at[pl.ds(0, tm * ROW_TILES)], rsem).wait()


def _dispatch(x1p, dest_flat, zrow, n_rows_alloc, tm):
    T = x1p.shape[0] // ROW_TILES
    grid_spec = pltpu.PrefetchScalarGridSpec(
        num_scalar_prefetch=1,
        grid=(T // tm,),
        in_specs=[
            pl.BlockSpec(memory_space=pl.ANY),
            pl.BlockSpec((tm * ROW_TILES, LANES), lambda i, z: (i, 0)),
        ],
        out_specs=pl.BlockSpec(memory_space=pl.ANY),
        scratch_shapes=[
            pltpu.SMEM((2 * tm * TOP_K,), jnp.int32),
            pltpu.VMEM((MOE_BLOCK * ROW_TILES, LANES), F32),
            pltpu.SemaphoreType.DMA((2,)),
            pltpu.SemaphoreType.DMA(()),
            pltpu.SemaphoreType.DMA(()),
        ],
    )
    return pl.pallas_call(
        _dispatch_kernel,
        grid_spec=grid_spec,
        out_shape=jax.ShapeDtypeStruct((n_rows_alloc * ROW_TILES, LANES), F32),
        compiler_params=_cparams(1),
        name="moe_dispatch",
    )(zrow, dest_flat, x1p)


def _expert_kernel(be_ref, nused_ref, xs_ref, wg_ref, wu_ref, wd_ref, y_ref, wg_s, wu_s, wd_s):
    i = pl.program_id(0)
    used = i < nused_ref[0]
    new_expert = (i == 0) | (be_ref[i] != be_ref[jnp.maximum(i - 1, 0)])

    @pl.when(used & new_expert)
    def _():
        wg_s[...] = wg_ref[0].astype(BF16)
        wu_s[...] = wu_ref[0].astype(BF16)
        wd_s[...] = wd_ref[0].astype(BF16)

    @pl.when(used)
    def _():
        xb = _load_rows(xs_ref).astype(BF16)
        hb = jax.nn.silu(_dot(xb, wg_s[...])) * _dot(xb, wu_s[...])
        y = _dot(hb.astype(BF16), wd_s[...])
        _store_rows(y_ref, y)

    @pl.when(jnp.logical_not(used))
    def _():
        y_ref[...] = jnp.zeros_like(y_ref)


def _experts(xs, block_e, n_used, w_eg, w_eu, w_ed, n_blocks):
    D = D_MODEL
    last_used = lambda i, nu: jnp.minimum(i, nu[0] - 1)
    row_block = (MOE_BLOCK * ROW_TILES, LANES)
    grid_spec = pltpu.PrefetchScalarGridSpec(
        num_scalar_prefetch=2,
        grid=(n_blocks,),
        in_specs=[
            pl.BlockSpec(row_block, lambda i, be, nu: (last_used(i, nu), 0)),
            pl.BlockSpec((1, D, D_EXPERT), lambda i, be, nu: (be[i], 0, 0)),
            pl.BlockSpec((1, D, D_EXPERT), lambda i, be, nu: (be[i], 0, 0)),
            pl.BlockSpec((1, D_EXPERT, D), lambda i, be, nu: (be[i], 0, 0)),
        ],
        out_specs=pl.BlockSpec(row_block, lambda i, be, nu: (i, 0)),
        scratch_shapes=[pltpu.VMEM((D, D_EXPERT), BF16), pltpu.VMEM((D, D_EXPERT), BF16),
                        pltpu.VMEM((D_EXPERT, D), BF16)],
    )
    return pl.pallas_call(
        _expert_kernel,
        grid_spec=grid_spec,
        out_shape=jax.ShapeDtypeStruct((n_blocks * MOE_BLOCK * ROW_TILES, LANES), F32),
        compiler_params=_cparams(1),
        name="expert_mlp",
    )(block_e, n_used, xs, w_eg, w_eu, w_ed)


def _ffn_ple_kernel(dest_hbm, y_hbm, x1_ref, gw_ref, p_ref, wsg_ref, wsu_ref, wsd_ref, fg_ref, fb_ref,
                    wpp_ref, wpg_ref, pg_ref, pb_ref, o_ref, idx_smem, ybuf, isem, rsem):
    i = pl.program_id(0)
    nsteps = pl.num_programs(0)
    tm = x1_ref.shape[0] // ROW_TILES
    n = tm * TOP_K

    def idx_copy(tile, slot):
        return pltpu.make_async_copy(dest_hbm.at[pl.ds(tile * n, n)],
                                     idx_smem.at[pl.ds(pl.multiple_of(slot * n, n), n)], isem.at[slot])

    def start_row_gathers(slot):
        base = slot * n

        def body(t, carry):
            for k in range(TOP_K):
                d = idx_smem[base + t * TOP_K + k]
                pltpu.make_async_copy(y_hbm.at[_tile_rows(d)], ybuf.at[slot, k, _tile_rows(t)],
                                      rsem.at[slot]).start(priority=k % 2)
            return carry

        lax.fori_loop(0, tm, body, 0)

    slot = i % 2
    nxt = 1 - slot

    @pl.when(i == 0)
    def _():
        idx_copy(0, 0).start()
        idx_copy(0, 0).wait()
        start_row_gathers(0)

        @pl.when(nsteps > 1)
        def _():
            idx_copy(1, 1).start()

    @pl.when(i + 1 < nsteps)
    def _():
        idx_copy(i + 1, nxt).wait()

        @pl.when(i + 2 < nsteps)
        def _():
            idx_copy(i + 2, slot).start()

        start_row_gathers(nxt)

    x1 = _load_rows(x1_ref)
    xb = x1.astype(BF16)
    hs = jax.nn.silu(_dot(xb, wsg_ref[...])) * _dot(xb, wsu_ref[...])
    shared = _dot(hs.astype(BF16), wsd_ref[...])

    for k in range(TOP_K):
        pltpu.make_async_copy(y_hbm.at[pl.ds(0, tm * ROW_TILES)], ybuf.at[slot, k], rsem.at[slot]).wait()
    gw = gw_ref[...]
    routed = jnp.zeros((tm, D_MODEL), F32)
    for k in range(TOP_K):
        routed = routed + gw[:, k:k + 1] * _load_rows(ybuf.at[slot, k])

    x2 = _layer_norm(DEEPNORM_ALPHA * x1 + (routed + shared), fg_ref[...], fb_ref[...])
    ple = _dot(p_ref[...].astype(BF16), wpp_ref[...]) * jax.nn.sigmoid(_dot(x2.astype(BF16), wpg_ref[...]))
    o_ref[...] = _layer_norm(DEEPNORM_ALPHA * x2 + ple, pg_ref[...], pb_ref[...])


def _ffn_ple(dest_flat, y, x1, gw_tok, p2, w_sg, w_su, w_sd, ffn_g, ffn_b, w_pp, w_pg, ple_g, ple_b, tm):
    T, D = x1.shape[0] // ROW_TILES, D_MODEL
    return pl.pallas_call(
        _ffn_ple_kernel,
        grid=(T // tm,),
        in_specs=[
            pl.BlockSpec(memory_space=pl.ANY),
            pl.BlockSpec(memory_space=pl.ANY),
            pl.BlockSpec((tm * ROW_TILES, LANES), lambda i: (i, 0)),
            pl.BlockSpec((tm, TOP_K), lambda i: (i, 0)),
            pl.BlockSpec((tm, PLE_DIM), lambda i: (i, 0)),
            _const_spec((D, D_SHARED)), _const_spec((D, D_SHARED)), _const_spec((D_SHARED, D)),
            _const_spec((1, D)), _const_spec((1, D)),
            _const_spec((PLE_DIM, D)), _const_spec((D, D)),
            _const_spec((1, D)), _const_spec((1, D)),
        ],
        out_specs=pl.BlockSpec((tm, D), lambda i: (i, 0)),
        out_shape=jax.ShapeDtypeStruct((T, D), F32),
        scratch_shapes=[
            pltpu.SMEM((2 * tm * TOP_K,), jnp.int32),
            pltpu.VMEM((2, TOP_K, tm * ROW_TILES, LANES), F32),
            pltpu.SemaphoreType.DMA((2,)),
            pltpu.SemaphoreType.DMA((2,)),
        ],
        compiler_params=_cparams(1),
        name="combine_ffn_ple_ln",
    )(dest_flat, y, x1, gw_tok, p2, w_sg, w_su, w_sd, ffn_g, ffn_b, w_pp, w_pg, ple_g, ple_b)


def _permute_w_in(w_in):
    D = w_in.shape[0]
    sizes = (ATTN_Q_W, ATTN_KV_W, ATTN_KV_W, M_QK_W, M_QK_W, M_V_W, M_V_W, M_HEADS, M_HEADS, D_MODEL, D_MODEL)
    offs = np.concatenate([[0], np.cumsum(sizes)])
    seg = [w_in[:, offs[k]:offs[k + 1]] for k in range(len(sizes))]
    aq, ak, av, mq, mk, mv, mo, mi, mf, ga, gb = seg
    aq = aq * (HEAD_DIM ** -0.5)
    mk = mk * (M_DQK ** -0.5)
    gates = jnp.concatenate([mi, mf, jnp.zeros((D, LANES - 2 * M_HEADS), w_in.dtype)], axis=1)
    pad = jnp.zeros((D, PROJ_COLS - (C_IF + LANES)), w_in.dtype)
    return jnp.concatenate([ga, gb, aq, mv, mo, mq, mk, ak, av, gates, pad], axis=1).astype(BF16)


def _route_layout(idx_t, rank_t, counts, T):
    n_pairs = T * TOP_K
    n_blocks = -(-n_pairs // MOE_BLOCK) + N_EXPERTS
    padded = (counts + MOE_BLOCK - 1) // MOE_BLOCK * MOE_BLOCK
    ends = jnp.cumsum(padded)
    pstarts = ends - padded
    onehot = idx_t[:, :, None] == jnp.arange(N_EXPERTS, dtype=jnp.int32)
    dest_t = jnp.sum(jnp.where(onehot, pstarts.astype(jnp.int32), 0), axis=-1) + rank_t
    dest_flat = dest_t.T.reshape(-1)
    block_row = jnp.arange(n_blocks, dtype=jnp.int32) * MOE_BLOCK
    block_e = jnp.minimum(jnp.sum(ends[None, :] <= block_row[:, None], axis=1), N_EXPERTS - 1).astype(jnp.int32)
    n_used = (ends[-1] // MOE_BLOCK).astype(jnp.int32).reshape(1)
    zrow = jnp.concatenate([pstarts + counts, ends, ends[-1:]]).astype(jnp.int32)
    return dest_flat, block_e, n_used, zrow, n_blocks


def kernel(x, p, ln_in_g, ln_in_b, w_in, attn_sinks, mlstm_b_i, mlstm_b_f, mlstm_norm_g, w_branch_attn, w_branch_mlstm, w_out, ln_mix_g, ln_mix_b, w_router, b_router, w_exp_gate, w_exp_up, w_exp_down, w_sh_gate, w_sh_up, w_sh_down, ln_ffn_g, ln_ffn_b, w_ple_proj, w_ple_gate, ln_ple_g, ln_ple_b):
    B, S, D = x.shape
    T = B * S
    assert D == D_MODEL and S % WINDOW == 0 and S % M_CHUNK == 0 and w_in.shape[0] == DEPTH
    x2 = x.reshape(T, D)
    row = lambda v: v.reshape(1, -1).astype(F32)

    tm_proj = min(1024, T)
    tm_mix = min(256, T)
    tm_router = min(512, T)
    tm_dispatch = min(512, T)
    tm_ffn = min(128, T)

    proj, gates = _in_projection(x2, row(ln_in_g), row(ln_in_b), _permute_w_in(w_in[0]), tm_proj)

    attn = _attention(proj, attn_sinks[0].astype(F32), B, S)

    gbias = jnp.concatenate([mlstm_b_i[0], mlstm_b_f[0], jnp.zeros((LANES - 2 * M_HEADS,), F32)]).reshape(1, LANES)
    mh = _mlstm(proj, gates, gbias, mlstm_norm_g[0].reshape(M_HEADS, 1, M_DV).astype(F32), B, S)

    x1 = _mix(attn, mh, proj, x2, row(ln_in_g), row(ln_in_b),
              w_branch_attn[0].astype(BF16), w_branch_mlstm[0].astype(BF16), w_out[0].astype(BF16),
              row(ln_mix_g[0]), row(ln_mix_b[0]), tm_mix)

    wr = w_router[0].astype(F32)
    wr_hi = wr.astype(BF16)
    wr_lo = (wr - wr_hi.astype(F32)).astype(BF16)
    w_a = jnp.concatenate([wr_hi, wr_lo], axis=1)
    w_b = jnp.concatenate([jnp.zeros_like(wr_hi), wr_hi], axis=1)
    idx_t, gw_t, rank_t, counts_f = _router(x1, w_a, w_b, b_router[0].reshape(N_EXPERTS, 1).astype(F32), tm_router)

    counts = counts_f[:, 0].astype(jnp.int32)
    dest_flat, block_e, n_used, zrow, n_blocks = _route_layout(idx_t, rank_t, counts, T)
    xs = _dispatch(x1, dest_flat, zrow, n_blocks * MOE_BLOCK, tm_dispatch)
    y = _experts(xs, block_e, n_used, w_exp_gate[0], w_exp_up[0], w_exp_down[0], n_blocks)

    out = _ffn_ple(dest_flat, y, x1, gw_t.T, p[0].reshape(T, PLE_DIM),
                   w_sh_gate[0].astype(BF16), w_sh_up[0].astype(BF16), w_sh_down[0].astype(BF16),
                   row(ln_ffn_g[0]), row(ln_ffn_b[0]),
                   w_ple_proj[0].astype(BF16), w_ple_gate[0].astype(BF16),
                   row(ln_ple_g[0]), row(ln_ple_b[0]), tm_ffn)
    return out.reshape(B, S, D)
```

```python
import functools

import numpy as np
import jax
import jax.numpy as jnp
from jax import lax
from jax.experimental import pallas as pl
from jax.experimental.pallas import tpu as pltpu

F32 = jnp.float32
BF16 = jnp.bfloat16

D_MODEL = 2048
PLE_DIM = 256
HEAD_DIM = 64
N_HEADS = 16
N_KV_HEADS = 2
GQA_GROUP = N_HEADS // N_KV_HEADS
WINDOW = 128
M_HEADS = 4
M_DV = 256
M_DQK = 128
M_CHUNK = 256
GATE_SOFTCAP = 15.0
N_EXPERTS = 64
TOP_K = 8
D_EXPERT = 512
D_SHARED = 512
ROUTED_SCALE = 2.5
MOE_BLOCK = 256
LN_EPS = 1e-5
RMS_EPS = 1e-6
DEPTH = 1
DEEPNORM_ALPHA = (2.0 * DEPTH) ** 0.25

ATTN_Q_W = N_HEADS * HEAD_DIM
ATTN_KV_W = N_KV_HEADS * HEAD_DIM
M_QK_W = M_HEADS * M_DQK
M_V_W = M_HEADS * M_DV

LANES = 128
NEG_BIG = -1e30

C_GA = 0
C_GB = C_GA + D_MODEL
C_AQ = C_GB + D_MODEL
C_MV = C_AQ + ATTN_Q_W
C_MO = C_MV + M_V_W
C_MQ = C_MO + M_V_W
C_MK = C_MQ + M_QK_W
C_AK = C_MK + M_QK_W
C_AV = C_AK + ATTN_KV_W
C_IF = C_AV + ATTN_KV_W
PROJ_TN = 512
PROJ_COLS = 8704
IF_TILE = C_IF // PROJ_TN
IF_OFF = C_IF - IF_TILE * PROJ_TN

SUBLANES = 8
ROW_PLANES = D_MODEL // (SUBLANES * LANES)
assert ROW_PLANES * SUBLANES * LANES == D_MODEL

VMEM_LIMIT = 52 * 1024 * 1024


def _cparams(n_axes):
    return pltpu.CompilerParams(dimension_semantics=("arbitrary",) * n_axes, vmem_limit_bytes=VMEM_LIMIT)


def _const_spec(shape):
    nd = len(shape)
    return pl.BlockSpec(shape, lambda *_: (0,) * nd, pipeline_mode=pl.Buffered(1))


def _layer_norm(x, g, b):
    mu = jnp.mean(x, axis=-1, keepdims=True)
    xc = x - mu
    var = jnp.mean(xc * xc, axis=-1, keepdims=True)
    return xc * lax.rsqrt(var + LN_EPS) * g + b


def _dot(a, b):
    return jnp.dot(a, b, preferred_element_type=F32)


def _dot_nt(a, b):
    return lax.dot_general(a, b, (((1,), (1,)), ((), ())), preferred_element_type=F32)


def _row_shape(rows):
    return (ROW_PLANES, rows * SUBLANES, LANES)


def _store_rows(ref, val):
    rows = val.shape[0]
    for h in range(ROW_PLANES):
        for c in range(SUBLANES):
            col = (h * SUBLANES + c) * LANES
            ref[h, pl.ds(c, rows, stride=SUBLANES), :] = val[:, col:col + LANES]


def _load_rows(ref):
    rows = ref.shape[1] // SUBLANES
    return jnp.concatenate([ref[h, pl.ds(c, rows, stride=SUBLANES), :]
                            for h in range(ROW_PLANES) for c in range(SUBLANES)], axis=1)


def _tile_rows(r, rows=1):
    return pl.ds(pl.multiple_of(r * SUBLANES, SUBLANES), rows * SUBLANES)


def _inproj_kernel(x_ref, g_ref, b_ref, w_ref, o_ref, og_ref, xn_ref):
    j = pl.program_id(1)

    @pl.when(j == 0)
    def _():
        xn_ref[...] = _layer_norm(x_ref[...], g_ref[...], b_ref[...]).astype(BF16)

    acc = _dot(xn_ref[...], w_ref[...])
    o_ref[...] = acc.astype(o_ref.dtype)

    @pl.when(j == IF_TILE)
    def _():
        og_ref[...] = acc[:, IF_OFF:IF_OFF + LANES]


def _in_projection(x2, ln_g, ln_b, w_perm, tm):
    T, D = x2.shape
    return pl.pallas_call(
        _inproj_kernel,
        grid=(T // tm, PROJ_COLS // PROJ_TN),
        in_specs=[
            pl.BlockSpec((tm, D), lambda i, j: (i, 0)),
            pl.BlockSpec((1, D), lambda i, j: (0, 0)),
            pl.BlockSpec((1, D), lambda i, j: (0, 0)),
            pl.BlockSpec((D, PROJ_TN), lambda i, j: (0, j)),
        ],
        out_specs=[
            pl.BlockSpec((tm, PROJ_TN), lambda i, j: (i, j)),
            pl.BlockSpec((tm, LANES), lambda i, j: (i, 0)),
        ],
        out_shape=[
            jax.ShapeDtypeStruct((T, PROJ_COLS), BF16),
            jax.ShapeDtypeStruct((T, LANES), F32),
        ],
        scratch_shapes=[pltpu.VMEM((tm, D), BF16)],
        compiler_params=_cparams(2),
        name="ln_inproj",
    )(x2, ln_g, ln_b, w_perm)


def _alibi_slope(h):
    return float(2.0 ** (-8.0 / N_HEADS * (h + 1)))


def _attn_kernel(sink_ref, q_ref, kp_ref, kc_ref, vp_ref, vc_ref, o_ref, *, blocks_per_seq):
    n = pl.program_id(0) % blocks_per_seq
    W = WINDOW
    qi = lax.broadcasted_iota(jnp.int32, (W, 2 * W), 0)
    kj = lax.broadcasted_iota(jnp.int32, (W, 2 * W), 1)
    dist = qi - kj + W
    valid = (dist >= 0) & (dist < W) & ((kj >= W) | (n > 0))
    neg_dist = jnp.where(valid, -dist.astype(F32), NEG_BIG)

    lane = lax.broadcasted_iota(jnp.int32, (2 * W, LANES), 1)
    lo = lane < HEAD_DIM
    lo_w = lax.broadcasted_iota(jnp.int32, (W, LANES), 1) < HEAD_DIM

    def dup_halves(prev_ref, cur_ref, g):
        a = jnp.concatenate([prev_ref[...], cur_ref[...]], axis=0).astype(F32)
        r = pltpu.roll(a, HEAD_DIM, 1)
        return jnp.where(lo, a, r) if g == 0 else jnp.where(lo, r, a)

    for g in range(N_KV_HEADS):
        kd = dup_halves(kp_ref, kc_ref, g)
        vd = dup_halves(vp_ref, vc_ref, g)
        zero = jnp.zeros_like(kd)
        kbd = jnp.concatenate([jnp.where(lo, kd, zero), jnp.where(lo, zero, kd)], axis=0).astype(BF16)
        vbd = jnp.concatenate([jnp.where(lo, vd, zero), jnp.where(lo, zero, vd)], axis=0).astype(BF16)
        for pp in range(GQA_GROUP // 2):
            pair = g * (GQA_GROUP // 2) + pp
            h_e, h_o = 2 * pair, 2 * pair + 1
            q2 = q_ref[:, pair * LANES:(pair + 1) * LANES]
            s = _dot_nt(q2, kbd)
            outs = []
            for half, h in ((0, h_e), (1, h_o)):
                sh = s[:, half * 2 * W:(half + 1) * 2 * W] + _alibi_slope(h) * neg_dist
                sink = sink_ref[h]
                m = jnp.maximum(jnp.max(sh, axis=1, keepdims=True), sink)
                p = jnp.exp(sh - m)
                l = jnp.sum(p, axis=1, keepdims=True) + jnp.exp(sink - m)
                outs.append((p, l))
            p_all = jnp.concatenate([outs[0][0], outs[1][0]], axis=1).astype(BF16)
            o2 = _dot(p_all, vbd)
            inv = jnp.where(lo_w, 1.0 / outs[0][1], 1.0 / outs[1][1])
            o_ref[:, pair * LANES:(pair + 1) * LANES] = (o2 * inv).astype(o_ref.dtype)


def _attention(proj, sinks, B, S):
    T = B * S
    nb = S // WINDOW
    kcol = C_AK // LANES
    vcol = C_AV // LANES
    kern = functools.partial(_attn_kernel, blocks_per_seq=nb)
    grid_spec = pltpu.PrefetchScalarGridSpec(
        num_scalar_prefetch=1,
        grid=(T // WINDOW,),
        in_specs=[
            pl.BlockSpec((WINDOW, ATTN_Q_W), lambda i, s: (i, C_AQ // ATTN_Q_W)),
            pl.BlockSpec((WINDOW, LANES), lambda i, s: (jnp.maximum(i - 1, 0), kcol)),
            pl.BlockSpec((WINDOW, LANES), lambda i, s: (i, kcol)),
            pl.BlockSpec((WINDOW, LANES), lambda i, s: (jnp.maximum(i - 1, 0), vcol)),
            pl.BlockSpec((WINDOW, LANES), lambda i, s: (i, vcol)),
        ],
        out_specs=pl.BlockSpec((WINDOW, ATTN_Q_W), lambda i, s: (i, 0)),
    )
    return pl.pallas_call(
        kern,
        grid_spec=grid_spec,
        out_shape=jax.ShapeDtypeStruct((T, ATTN_Q_W), BF16),
        compiler_params=_cparams(1),
        name="swa_attention",
    )(sinks, proj, proj, proj, proj, proj)


def _soft_cap(z):
    return GATE_SOFTCAP * jnp.tanh(z / GATE_SOFTCAP)


def _mlstm_kernel(q_ref, k_ref, v_ref, og_ref, gates_ref, gbias_ref, normg_ref, o_ref, c_ref, n_ref, m_ref):
    L = M_CHUNK
    n_seq = q_ref.shape[0]

    @pl.when(pl.program_id(0) == 0)
    def _():
        c_ref[...] = jnp.zeros_like(c_ref)
        n_ref[...] = jnp.zeros_like(n_ref)
        m_ref[...] = jnp.zeros_like(m_ref)

    ti = lax.broadcasted_iota(jnp.int32, (L, L), 0)
    si = lax.broadcasted_iota(jnp.int32, (L, L), 1)
    causal = si <= ti
    tril = causal.astype(F32)

    work = []
    for b in range(n_seq):
        sc = _soft_cap(gates_ref[b] + gbias_ref[...])
        lf = jax.nn.log_sigmoid(sc)
        bcum = jnp.dot(tril, lf, precision=lax.Precision.HIGHEST, preferred_element_type=F32)
        sc_t = sc.T
        bcum_t = bcum.T
        for h in range(M_HEADS):
            st = b * M_HEADS + h
            work.append(dict(
                b=b, h=h, st=st,
                ig_col=sc[:, h:h + 1], ig_row=sc_t[h:h + 1, :],
                b_col=bcum[:, M_HEADS + h:M_HEADS + h + 1], b_row=bcum_t[M_HEADS + h:M_HEADS + h + 1, :],
                m_prev=m_ref[st][:, :1], C=c_ref[st], nvec=n_ref[st],
                qx=q_ref[b, :, h * M_DQK:(h + 1) * M_DQK], kx=k_ref[b, :, h * M_DQK:(h + 1) * M_DQK],
                vx=v_ref[b, :, h * M_DV:(h + 1) * M_DV], og=og_ref[b, :, h * M_DV:(h + 1) * M_DV],
                normg=normg_ref[h]))

    results = []
    for wk in work:
        ig_col, ig_row, b_col, b_row = wk["ig_col"], wk["ig_row"], wk["b_col"], wk["b_row"]
        m_prev, C, nvec, qx, kx, vx = wk["m_prev"], wk["C"], wk["nvec"], wk["qx"], wk["kx"], wk["vx"]

        inter = b_col + m_prev
        dmat = jnp.where(causal, b_col - b_row + ig_row, NEG_BIG)
        mt = jnp.maximum(inter, jnp.max(dmat, axis=1, keepdims=True))
        w = jnp.exp(dmat - mt)
        sqk = _dot_nt(qx, kx) * w
        s_i = jnp.exp(inter - mt)
        num = s_i * _dot(qx, C.astype(BF16)) + _dot(sqk.astype(BF16), vx)
        nq = s_i * jnp.sum(qx.astype(F32) * nvec, axis=1, keepdims=True) + jnp.sum(sqk, axis=1, keepdims=True)
        hh = num / jnp.maximum(jnp.abs(nq), jnp.exp(-mt))

        bl = b_col[L - 1:L, :]
        g_col = bl - b_col + ig_col
        m_new = jnp.maximum(bl + m_prev, jnp.max(g_col, axis=0, keepdims=True))
        decay = jnp.exp(bl + m_prev - m_new)
        wg = jnp.exp(g_col - m_new)
        kw = kx.astype(F32) * wg
        c_new = decay * C + _dot(kw.T.astype(BF16), vx)
        n_new = decay * nvec + jnp.sum(kw, axis=0, keepdims=True)

        hn = hh * lax.rsqrt(jnp.mean(hh * hh, axis=1, keepdims=True) + RMS_EPS) * wk["normg"]
        out = (hn * jax.nn.sigmoid(wk["og"].astype(F32))).astype(o_ref.dtype)
        results.append((c_new, n_new, m_new, out))

    for wk, (c_new, n_new, m_new, out) in zip(work, results):
        st, b, h = wk["st"], wk["b"], wk["h"]
        c_ref[st] = c_new
        n_ref[st] = n_new
        m_ref[st] = jnp.broadcast_to(m_new, (1, LANES))
        o_ref[b, :, h * M_DV:(h + 1) * M_DV] = out


def _mlstm(proj, gates, gbias, norm_g, B, S):
    nc = S // M_CHUNK
    L = M_CHUNK
    proj3 = proj.reshape(B, S, PROJ_COLS)
    gates3 = gates.reshape(B, S, LANES)
    out = pl.pallas_call(
        _mlstm_kernel,
        grid=(nc,),
        in_specs=[
            pl.BlockSpec((B, L, M_QK_W), lambda c: (0, c, C_MQ // M_QK_W)),
            pl.BlockSpec((B, L, M_QK_W), lambda c: (0, c, C_MK // M_QK_W)),
            pl.BlockSpec((B, L, M_V_W), lambda c: (0, c, C_MV // M_V_W)),
            pl.BlockSpec((B, L, M_V_W), lambda c: (0, c, C_MO // M_V_W)),
            pl.BlockSpec((B, L, LANES), lambda c: (0, c, 0)),
            pl.BlockSpec((1, LANES), lambda c: (0, 0)),
            pl.BlockSpec((M_HEADS, 1, M_DV), lambda c: (0, 0, 0)),
        ],
        out_specs=pl.BlockSpec((B, L, M_V_W), lambda c: (0, c, 0)),
        out_shape=jax.ShapeDtypeStruct((B, S, M_V_W), BF16),
        scratch_shapes=[
            pltpu.VMEM((B * M_HEADS, M_DQK, M_DV), F32),
            pltpu.VMEM((B * M_HEADS, 1, M_DQK), F32),
            pltpu.VMEM((B * M_HEADS, 1, LANES), F32),
        ],
        compiler_params=_cparams(1),
        name="mlstm_chunkwise",
    )(proj3, proj3, proj3, proj3, gates3, gbias, norm_g)
    return out.reshape(B * S, M_V_W)


def _mix_kernel(attn_ref, mh_ref, ga_ref, gb_ref, x_ref, lng_ref, lnb_ref, wba_ref, wbm_ref, wout_ref,
                mg_ref, mb_ref, o_ref):
    a = _dot(attn_ref[...], wba_ref[...])
    m = _dot(mh_ref[...], wbm_ref[...])
    merged = jax.nn.sigmoid(ga_ref[...].astype(F32)) * a + jax.nn.sigmoid(gb_ref[...].astype(F32)) * m
    y = _dot(merged.astype(BF16), wout_ref[...])
    h0 = _layer_norm(x_ref[...], lng_ref[...], lnb_ref[...])
    x1 = _layer_norm(DEEPNORM_ALPHA * h0 + y, mg_ref[...], mb_ref[...])
    _store_rows(o_ref, x1)


def _mix(attn, mh, proj, x2, ln_g, ln_b, w_ba, w_bm, w_out, mix_g, mix_b, tm):
    T, D = x2.shape
    return pl.pallas_call(
        _mix_kernel,
        grid=(T // tm,),
        in_specs=[
            pl.BlockSpec((tm, ATTN_Q_W), lambda i: (i, 0)),
            pl.BlockSpec((tm, M_V_W), lambda i: (i, 0)),
            pl.BlockSpec((tm, D), lambda i: (i, C_GA // D_MODEL)),
            pl.BlockSpec((tm, D), lambda i: (i, C_GB // D_MODEL)),
            pl.BlockSpec((tm, D), lambda i: (i, 0)),
            _const_spec((1, D)), _const_spec((1, D)),
            _const_spec((ATTN_Q_W, D)), _const_spec((M_V_W, D)), _const_spec((D, D)),
            _const_spec((1, D)), _const_spec((1, D)),
        ],
        out_specs=pl.BlockSpec(_row_shape(tm), lambda i: (0, i, 0)),
        out_shape=jax.ShapeDtypeStruct(_row_shape(T), F32),
        compiler_params=_cparams(1),
        name="mix_outproj_ln",
    )(attn, mh, proj, proj, x2, ln_g, ln_b, w_ba, w_bm, w_out, mix_g, mix_b)


def _router_kernel(x_ref, wa_ref, wb_ref, br_ref, idx_ref, gw_ref, rank_ref, cnt_ref, carry_ref):
    i = pl.program_id(0)
    tm = x_ref.shape[1] // SUBLANES
    E = N_EXPERTS

    @pl.when(i == 0)
    def _():
        carry_ref[...] = jnp.zeros_like(carry_ref)

    x = _load_rows(x_ref)
    xh = x.astype(BF16)
    xl = (x - xh.astype(F32)).astype(BF16)
    c = _dot(xh, wa_ref[...]) + _dot(xl, wb_ref[...])
    logits_t = (c + pltpu.roll(c, E, 1)).T[:E]
    scores = jax.nn.sigmoid(logits_t)
    sel = scores + br_ref[...]
    erow = lax.broadcasted_iota(jnp.int32, (E, tm), 0).astype(F32)

    idx_rows, g_rows, hits = [], [], []
    for _ in range(TOP_K):
        mx = jnp.max(sel, axis=0, keepdims=True)
        am = jnp.min(jnp.where(sel == mx, erow, float(E)), axis=0, keepdims=True)
        hit = erow == am
        g_rows.append(jnp.sum(jnp.where(hit, scores, 0.0), axis=0, keepdims=True))
        sel = jnp.where(hit, -jnp.inf, sel)
        idx_rows.append(am)
        hits.append(hit)
    gsum = functools.reduce(lambda a, b: a + b, g_rows)
    chosen = functools.reduce(lambda a, b: a + b, [h.astype(F32) for h in hits])

    r_i = lax.broadcasted_iota(jnp.int32, (tm, tm), 0)
    c_i = lax.broadcasted_iota(jnp.int32, (tm, tm), 1)
    before = (r_i < c_i).astype(BF16)
    prefix = _dot(chosen.astype(BF16), before) + carry_ref[:, :1]
    rank_rows = [jnp.sum(jnp.where(h, prefix, 0.0), axis=0, keepdims=True) for h in hits]

    krow = lax.broadcasted_iota(jnp.int32, (TOP_K, tm), 0)

    def stack(rows):
        out = jnp.zeros((TOP_K, tm), F32)
        for k, r in enumerate(rows):
            out = jnp.where(krow == k, r, out)
        return out

    idx_ref[...] = stack(idx_rows).astype(jnp.int32)
    gw_ref[...] = stack(g_rows) / gsum * ROUTED_SCALE
    rank_ref[...] = stack(rank_rows).astype(jnp.int32)
    carry_ref[...] = carry_ref[...] + jnp.sum(chosen, axis=1, keepdims=True)
    cnt_ref[...] = carry_ref[...]


def _router(x1, w_a, w_b, b_col, tm):
    T, D = x1.shape[1] // SUBLANES, D_MODEL
    tok = lambda i: (0, i)
    return pl.pallas_call(
        _router_kernel,
        grid=(T // tm,),
        in_specs=[
            pl.BlockSpec(_row_shape(tm), lambda i: (0, i, 0)),
            _const_spec((D, 2 * N_EXPERTS)),
            _const_spec((D, 2 * N_EXPERTS)),
            _const_spec((N_EXPERTS, 1)),
        ],
        out_specs=[pl.BlockSpec((TOP_K, tm), tok), pl.BlockSpec((TOP_K, tm), tok), pl.BlockSpec((TOP_K, tm), tok),
                   pl.BlockSpec((N_EXPERTS, LANES), lambda i: (0, 0))],
        out_shape=[jax.ShapeDtypeStruct((TOP_K, T), jnp.int32), jax.ShapeDtypeStruct((TOP_K, T), F32),
                   jax.ShapeDtypeStruct((TOP_K, T), jnp.int32), jax.ShapeDtypeStruct((N_EXPERTS, LANES), F32)],
        scratch_shapes=[pltpu.VMEM((N_EXPERTS, LANES), F32)],
        compiler_params=_cparams(1),
        name="router_topk_rank",
    )(x1, w_a, w_b, b_col)


ZFILL_SIZES = tuple(2 ** b for b in reversed(range(MOE_BLOCK.bit_length() - 1)))
assert MOE_BLOCK & (MOE_BLOCK - 1) == 0


def _dispatch_kernel(zrow_ref, dest_hbm, xp_ref, xs_hbm, idx_smem, zbuf, isem, zsem, rsem):
    i = pl.program_id(0)
    nsteps = pl.num_programs(0)
    tm = xp_ref.shape[1] // SUBLANES
    n = tm * TOP_K

    def idx_copy(tile, slot):
        return pltpu.make_async_copy(dest_hbm.at[pl.ds(tile * n, n)],
                                     idx_smem.at[pl.ds(pl.multiple_of(slot * n, n), n)], isem.at[slot])

    def zero_copy(row, rows):
        return pltpu.make_async_copy(zbuf.at[:, pl.ds(0, rows * SUBLANES)], xs_hbm.at[:, _tile_rows(row, rows)], zsem)

    def zero_fill_plan():
        plan = []
        for e in range(N_EXPERTS):
            start = zrow_ref[e]
            rem = zrow_ref[N_EXPERTS + e] - start
            for rows in ZFILL_SIZES:
                take = rem >= rows
                plan.append((take, start, rows))
                start = jnp.where(take, start + rows, start)
                rem = jnp.where(take, rem - rows, rem)
        n_alloc = xs_hbm.shape[1] // SUBLANES
        for j in range(N_EXPERTS + 2):
            s = zrow_ref[2 * N_EXPERTS] + j * MOE_BLOCK
            plan.append((s < n_alloc, jnp.minimum(s, n_alloc - MOE_BLOCK), MOE_BLOCK))
        return plan

    @pl.when(i == 0)
    def _():
        idx_copy(0, 0).start()
        zbuf[...] = jnp.zeros_like(zbuf)
        plan = zero_fill_plan()
        for take, start, rows in plan:
            @pl.when(take)
            def _():
                zero_copy(start, rows).start()
        for take, start, rows in plan:
            @pl.when(take)
            def _():
                zero_copy(0, rows).wait()

    slot = i % 2
    idx_copy(i, slot).wait()

    @pl.when(i + 1 < nsteps)
    def _():
        idx_copy(i + 1, 1 - slot).start()

    base = slot * n

    def body(t, carry):
        for k in range(TOP_K):
            d = idx_smem[base + t * TOP_K + k]
            pltpu.make_async_copy(xp_ref.at[:, _tile_rows(t)], xs_hbm.at[:, _tile_rows(d)],
                                  rsem).start(priority=k % 2)
        return carry

    lax.fori_loop(0, tm, body, 0)
    for _ in range(TOP_K):
        pltpu.make_async_copy(xp_ref, xs_hbm.at[:, _tile_rows(0, tm)], rsem).wait()


def _dispatch(x1p, dest_flat, zrow, n_rows_alloc, tm):
    T = x1p.shape[1] // SUBLANES
    grid_spec = pltpu.PrefetchScalarGridSpec(
        num_scalar_prefetch=1,
        grid=(T // tm,),
        in_specs=[
            pl.BlockSpec(memory_space=pl.ANY),
            pl.BlockSpec(_row_shape(tm), lambda i, z: (0, i, 0)),
        ],
        out_specs=pl.BlockSpec(memory_space=pl.ANY),
        scratch_shapes=[
            pltpu.SMEM((2 * tm * TOP_K,), jnp.int32),
            pltpu.VMEM(_row_shape(MOE_BLOCK), F32),
            pltpu.SemaphoreType.DMA((2,)),
            pltpu.SemaphoreType.DMA(()),
            pltpu.SemaphoreType.DMA(()),
        ],
    )
    return pl.pallas_call(
        _dispatch_kernel,
        grid_spec=grid_spec,
        out_shape=jax.ShapeDtypeStruct(_row_shape(n_rows_alloc), F32),
        compiler_params=_cparams(1),
        name="moe_dispatch",
    )(zrow, dest_flat, x1p)


def _expert_kernel(be_ref, nused_ref, xs_ref, wg_ref, wu_ref, wd_ref, y_ref, wg_s, wu_s, wd_s):
    i = pl.program_id(0)
    used = i < nused_ref[0]
    new_expert = (i == 0) | (be_ref[i] != be_ref[jnp.maximum(i - 1, 0)])

    @pl.when(used & new_expert)
    def _():
        wg_s[...] = wg_ref[0].astype(BF16)
        wu_s[...] = wu_ref[0].astype(BF16)
        wd_s[...] = wd_ref[0].astype(BF16)

    @pl.when(used)
    def _():
        xb = _load_rows(xs_ref).astype(BF16)
        hb = jax.nn.silu(_dot(xb, wg_s[...])) * _dot(xb, wu_s[...])
        y = _dot(hb.astype(BF16), wd_s[...])
        _store_rows(y_ref, y)

    @pl.when(jnp.logical_not(used))
    def _():
        y_ref[...] = jnp.zeros_like(y_ref)


def _experts(xs, block_e, n_used, w_eg, w_eu, w_ed, n_blocks):
    D = D_MODEL
    last_used = lambda i, nu: jnp.minimum(i, nu[0] - 1)
    row_block = _row_shape(MOE_BLOCK)
    grid_spec = pltpu.PrefetchScalarGridSpec(
        num_scalar_prefetch=2,
        grid=(n_blocks,),
        in_specs=[
            pl.BlockSpec(row_block, lambda i, be, nu: (0, last_used(i, nu), 0)),
            pl.BlockSpec((1, D, D_EXPERT), lambda i, be, nu: (be[i], 0, 0)),
            pl.BlockSpec((1, D, D_EXPERT), lambda i, be, nu: (be[i], 0, 0)),
            pl.BlockSpec((1, D_EXPERT, D), lambda i, be, nu: (be[i], 0, 0)),
        ],
        out_specs=pl.BlockSpec(row_block, lambda i, be, nu: (0, i, 0)),
        scratch_shapes=[pltpu.VMEM((D, D_EXPERT), BF16), pltpu.VMEM((D, D_EXPERT), BF16),
                        pltpu.VMEM((D_EXPERT, D), BF16)],
    )
    return pl.pallas_call(
        _expert_kernel,
        grid_spec=grid_spec,
        out_shape=jax.ShapeDtypeStruct(_row_shape(n_blocks * MOE_BLOCK), F32),
        compiler_params=_cparams(1),
        name="expert_mlp",
    )(block_e, n_used, xs, w_eg, w_eu, w_ed)


def _combine_kernel(dest_hbm, y_hbm, gw_ref, o_ref, idx_smem, ybuf, isem, rsem):
    i = pl.program_id(0)
    nsteps = pl.num_programs(0)
    tm = o_ref.shape[0]
    n = tm * TOP_K

    def idx_copy(tile, slot):
        return pltpu.make_async_copy(dest_hbm.at[pl.ds(tile * n, n)],
                                     idx_smem.at[pl.ds(pl.multiple_of(slot * n, n), n)], isem.at[slot])

    def start_row_gathers(slot):
        base = slot * n

        def body(t, carry):
            for k in range(TOP_K):
                d = idx_smem[base + t * TOP_K + k]
                pltpu.make_async_copy(y_hbm.at[:, _tile_rows(d)], ybuf.at[slot, k, :, _tile_rows(t)],
                                      rsem.at[slot]).start(priority=k % 2)
            return carry

        lax.fori_loop(0, tm, body, 0)

    slot = i % 2
    nxt = 1 - slot

    @pl.when(i == 0)
    def _():
        idx_copy(0, 0).start()
        idx_copy(0, 0).wait()
        start_row_gathers(0)

        @pl.when(nsteps > 1)
        def _():
            idx_copy(1, 1).start()

    @pl.when(i + 1 < nsteps)
    def _():
        idx_copy(i + 1, nxt).wait()

        @pl.when(i + 2 < nsteps)
        def _():
            idx_copy(i + 2, slot).start()

        start_row_gathers(nxt)

    for k in range(TOP_K):
        pltpu.make_async_copy(y_hbm.at[:, _tile_rows(0, tm)], ybuf.at[slot, k], rsem.at[slot]).wait()
    gw = gw_ref[...]
    routed = jnp.zeros((tm, D_MODEL), F32)
    for k in range(TOP_K):
        routed = routed + gw[:, k:k + 1] * _load_rows(ybuf.at[slot, k])
    o_ref[...] = routed


def _combine(dest_flat, y, gw_tok, tm):
    T = gw_tok.shape[0]
    return pl.pallas_call(
        _combine_kernel,
        grid=(T // tm,),
        in_specs=[
            pl.BlockSpec(memory_space=pl.ANY),
            pl.BlockSpec(memory_space=pl.ANY),
            pl.BlockSpec((tm, TOP_K), lambda i: (i, 0)),
        ],
        out_specs=pl.BlockSpec((tm, D_MODEL), lambda i: (i, 0)),
        out_shape=jax.ShapeDtypeStruct((T, D_MODEL), F32),
        scratch_shapes=[
            pltpu.SMEM((2 * tm * TOP_K,), jnp.int32),
            pltpu.VMEM((2, TOP_K) + _row_shape(tm), F32),
            pltpu.SemaphoreType.DMA((2,)),
            pltpu.SemaphoreType.DMA((2,)),
        ],
        compiler_params=_cparams(1),
        name="moe_combine",
    )(dest_flat, y, gw_tok)


def _ffn_ple_kernel(x1_ref, yr_ref, p_ref, wsg_ref, wsu_ref, wsd_ref, fg_ref, fb_ref, wpp_ref, wpg_ref,
                    pg_ref, pb_ref, o_ref):
    x1 = _load_rows(x1_ref)
    xb = x1.astype(BF16)
    hs = jax.nn.silu(_dot(xb, wsg_ref[...])) * _dot(xb, wsu_ref[...])
    shared = _dot(hs.astype(BF16), wsd_ref[...])
    x2 = _layer_norm(DEEPNORM_ALPHA * x1 + (yr_ref[...] + shared), fg_ref[...], fb_ref[...])
    ple = _dot(p_ref[...].astype(BF16), wpp_ref[...]) * jax.nn.sigmoid(_dot(x2.astype(BF16), wpg_ref[...]))
    o_ref[...] = _layer_norm(DEEPNORM_ALPHA * x2 + ple, pg_ref[...], pb_ref[...])


def _ffn_ple(x1, y_routed, p2, w_sg, w_su, w_sd, ffn_g, ffn_b, w_pp, w_pg, ple_g, ple_b, tm):
    T, D = y_routed.shape
    return pl.pallas_call(
        _ffn_ple_kernel,
        grid=(T // tm,),
        in_specs=[
            pl.BlockSpec(_row_shape(tm), lambda i: (0, i, 0)),
            pl.BlockSpec((tm, D), lambda i: (i, 0)),
            pl.BlockSpec((tm, PLE_DIM), lambda i: (i, 0)),
            _const_spec((D, D_SHARED)), _const_spec((D, D_SHARED)), _const_spec((D_SHARED, D)),
            _const_spec((1, D)), _const_spec((1, D)),
            _const_spec((PLE_DIM, D)), _const_spec((D, D)),
            _const_spec((1, D)), _const_spec((1, D)),
        ],
        out_specs=pl.BlockSpec((tm, D), lambda i: (i, 0)),
        out_shape=jax.ShapeDtypeStruct((T, D), F32),
        compiler_params=_cparams(1),
        name="shared_ffn_ple_ln",
    )(x1, y_routed, p2, w_sg, w_su, w_sd, ffn_g, ffn_b, w_pp, w_pg, ple_g, ple_b)


def _permute_w_in(w_in):
    D = w_in.shape[0]
    sizes = (ATTN_Q_W, ATTN_KV_W, ATTN_KV_W, M_QK_W, M_QK_W, M_V_W, M_V_W, M_HEADS, M_HEADS, D_MODEL, D_MODEL)
    offs = np.concatenate([[0], np.cumsum(sizes)])
    seg = [w_in[:, offs[k]:offs[k + 1]] for k in range(len(sizes))]
    aq, ak, av, mq, mk, mv, mo, mi, mf, ga, gb = seg
    aq = aq * (HEAD_DIM ** -0.5)
    mk = mk * (M_DQK ** -0.5)
    gates = jnp.concatenate([mi, mf, jnp.zeros((D, LANES - 2 * M_HEADS), w_in.dtype)], axis=1)
    pad = jnp.zeros((D, PROJ_COLS - (C_IF + LANES)), w_in.dtype)
    return jnp.concatenate([ga, gb, aq, mv, mo, mq, mk, ak, av, gates, pad], axis=1).astype(BF16)


def _route_layout(idx_t, rank_t, counts, T):
    n_pairs = T * TOP_K
    n_blocks = -(-n_pairs // MOE_BLOCK) + N_EXPERTS
    padded = (counts + MOE_BLOCK - 1) // MOE_BLOCK * MOE_BLOCK
    ends = jnp.cumsum(padded)
    pstarts = ends - padded
    onehot = idx_t[:, :, None] == jnp.arange(N_EXPERTS, dtype=jnp.int32)
    dest_t = jnp.sum(jnp.where(onehot, pstarts.astype(jnp.int32), 0), axis=-1) + rank_t
    dest_flat = dest_t.T.reshape(-1)
    block_row = jnp.arange(n_blocks, dtype=jnp.int32) * MOE_BLOCK
    block_e = jnp.minimum(jnp.sum(ends[None, :] <= block_row[:, None], axis=1), N_EXPERTS - 1).astype(jnp.int32)
    n_used = (ends[-1] // MOE_BLOCK).astype(jnp.int32).reshape(1)
    zrow = jnp.concatenate([pstarts + counts, ends, ends[-1:]]).astype(jnp.int32)
    return dest_flat, block_e, n_used, zrow, n_blocks


def kernel(x, p, ln_in_g, ln_in_b, w_in, attn_sinks, mlstm_b_i, mlstm_b_f, mlstm_norm_g, w_branch_attn, w_branch_mlstm, w_out, ln_mix_g, ln_mix_b, w_router, b_router, w_exp_gate, w_exp_up, w_exp_down, w_sh_gate, w_sh_up, w_sh_down, ln_ffn_g, ln_ffn_b, w_ple_proj, w_ple_gate, ln_ple_g, ln_ple_b):
    B, S, D = x.shape
    T = B * S
    assert D == D_MODEL and S % WINDOW == 0 and S % M_CHUNK == 0 and w_in.shape[0] == DEPTH
    x2 = x.reshape(T, D)
    row = lambda v: v.reshape(1, -1).astype(F32)

    tm_proj = min(1024, T)
    tm_mix = min(256, T)
    tm_router = min(512, T)
    tm_dispatch = min(512, T)
    tm_combine = min(256, T)
    tm_ffn = min(256, T)

    proj, gates = _in_projection(x2, row(ln_in_g), row(ln_in_b), _permute_w_in(w_in[0]), tm_proj)

    attn = _attention(proj, attn_sinks[0].astype(F32), B, S)

    gbias = jnp.concatenate([mlstm_b_i[0], mlstm_b_f[0], jnp.zeros((LANES - 2 * M_HEADS,), F32)]).reshape(1, LANES)
    mh = _mlstm(proj, gates, gbias, mlstm_norm_g[0].reshape(M_HEADS, 1, M_DV).astype(F32), B, S)

    x1 = _mix(attn, mh, proj, x2, row(ln_in_g), row(ln_in_b),
              w_branch_attn[0].astype(BF16), w_branch_mlstm[0].astype(BF16), w_out[0].astype(BF16),
              row(ln_mix_g[0]), row(ln_mix_b[0]), tm_mix)

    wr = w_router[0].astype(F32)
    wr_hi = wr.astype(BF16)
    wr_lo = (wr - wr_hi.astype(F32)).astype(BF16)
    w_a = jnp.concatenate([wr_hi, wr_lo], axis=1)
    w_b = jnp.concatenate([jnp.zeros_like(wr_hi), wr_hi], axis=1)
    idx_t, gw_t, rank_t, counts_f = _router(x1, w_a, w_b, b_router[0].reshape(N_EXPERTS, 1).astype(F32), tm_router)

    counts = counts_f[:, 0].astype(jnp.int32)
    dest_flat, block_e, n_used, zrow, n_blocks = _route_layout(idx_t, rank_t, counts, T)
    xs = _dispatch(x1, dest_flat, zrow, n_blocks * MOE_BLOCK, tm_dispatch)
    y = _experts(xs, block_e, n_used, w_exp_gate[0], w_exp_up[0], w_exp_down[0], n_blocks)

    y_routed = _combine(dest_flat, y, gw_t.T, tm_combine)
    out = _ffn_ple(x1, y_routed, p[0].reshape(T, PLE_DIM),
                   w_sh_gate[0].astype(BF16), w_sh_up[0].astype(BF16), w_sh_down[0].astype(BF16),
                   row(ln_ffn_g[0]), row(ln_ffn_b[0]),
                   w_ple_proj[0].astype(BF16), w_ple_gate[0].astype(BF16),
                   row(ln_ple_g[0]), row(ln_ple_b[0]), tm_ffn)
    return out.reshape(B, S, D)
```

```python
import functools

import numpy as np
import jax
import jax.numpy as jnp
from jax import lax
from jax.experimental import pallas as pl
from jax.experimental.pallas import tpu as pltpu

F32 = jnp.float32
BF16 = jnp.bfloat16

D_MODEL = 2048
PLE_DIM = 256
HEAD_DIM = 64
N_HEADS = 16
N_KV_HEADS = 2
GQA_GROUP = N_HEADS // N_KV_HEADS
WINDOW = 128
M_HEADS = 4
M_DV = 256
M_DQK = 128
M_CHUNK = 256
GATE_SOFTCAP = 15.0
N_EXPERTS = 64
TOP_K = 8
D_EXPERT = 512
D_SHARED = 512
ROUTED_SCALE = 2.5
MOE_BLOCK = 512
LN_EPS = 1e-5
RMS_EPS = 1e-6
DEPTH = 1
DEEPNORM_ALPHA = (2.0 * DEPTH) ** 0.25

ATTN_Q_W = N_HEADS * HEAD_DIM
ATTN_KV_W = N_KV_HEADS * HEAD_DIM
M_QK_W = M_HEADS * M_DQK
M_V_W = M_HEADS * M_DV

LANES = 128
NEG_BIG = -1e30

C_GA = 0
C_GB = C_GA + D_MODEL
C_AQ = C_GB + D_MODEL
C_MV = C_AQ + ATTN_Q_W
C_MO = C_MV + M_V_W
C_MQ = C_MO + M_V_W
C_MK = C_MQ + M_QK_W
C_AK = C_MK + M_QK_W
C_AV = C_AK + ATTN_KV_W
PROJ_COLS = C_AV + ATTN_KV_W
PROJ_TN = 1408
assert PROJ_COLS % PROJ_TN == 0 and PROJ_TN % LANES == 0

SUBLANES = 8
ROW_PLANES = D_MODEL // (SUBLANES * LANES)
assert ROW_PLANES * SUBLANES * LANES == D_MODEL

VMEM_LIMIT = 56 * 1024 * 1024


def _cparams(n_axes):
    return pltpu.CompilerParams(dimension_semantics=("arbitrary",) * n_axes, vmem_limit_bytes=VMEM_LIMIT)


def _const_spec(shape):
    nd = len(shape)
    return pl.BlockSpec(shape, lambda *_: (0,) * nd, pipeline_mode=pl.Buffered(1))


def _layer_norm(x, g, b):
    mu = jnp.mean(x, axis=-1, keepdims=True)
    xc = x - mu
    var = jnp.mean(xc * xc, axis=-1, keepdims=True)
    return xc * lax.rsqrt(var + LN_EPS) * g + b


def _dot(a, b):
    return jnp.dot(a, b, preferred_element_type=F32)


def _dot_nt(a, b):
    return lax.dot_general(a, b, (((1,), (1,)), ((), ())), preferred_element_type=F32)


def _row_shape(rows):
    return (ROW_PLANES, rows * SUBLANES, LANES)


def _store_rows(ref, val):
    rows = val.shape[0]
    for h in range(ROW_PLANES):
        for c in range(SUBLANES):
            col = (h * SUBLANES + c) * LANES
            ref[h, pl.ds(c, rows, stride=SUBLANES), :] = val[:, col:col + LANES]


def _load_rows(ref):
    rows = ref.shape[1] // SUBLANES
    return jnp.concatenate([ref[h, pl.ds(c, rows, stride=SUBLANES), :]
                            for h in range(ROW_PLANES) for c in range(SUBLANES)], axis=1)


def _tile_rows(r, rows=1):
    return pl.ds(pl.multiple_of(r * SUBLANES, SUBLANES), rows * SUBLANES)


def _inproj_kernel(x_ref, g_ref, b_ref, w_ref, wif_ref, o_ref, og_ref, xn_ref):
    j = pl.program_id(1)

    @pl.when(j == 0)
    def _():
        xn = _layer_norm(x_ref[...], g_ref[...], b_ref[...]).astype(BF16)
        xn_ref[...] = xn
        og_ref[...] = _dot(xn, wif_ref[...])

    o_ref[...] = _dot(xn_ref[...], w_ref[...]).astype(o_ref.dtype)


def _in_projection(x2, ln_g, ln_b, w_perm, w_if, tm):
    T, D = x2.shape
    return pl.pallas_call(
        _inproj_kernel,
        grid=(T // tm, PROJ_COLS // PROJ_TN),
        in_specs=[
            pl.BlockSpec((tm, D), lambda i, j: (i, 0)),
            pl.BlockSpec((1, D), lambda i, j: (0, 0)),
            pl.BlockSpec((1, D), lambda i, j: (0, 0)),
            pl.BlockSpec((D, PROJ_TN), lambda i, j: (0, j)),
            pl.BlockSpec((D, LANES), lambda i, j: (0, 0)),
        ],
        out_specs=[
            pl.BlockSpec((tm, PROJ_TN), lambda i, j: (i, j)),
            pl.BlockSpec((tm, LANES), lambda i, j: (i, 0)),
        ],
        out_shape=[
            jax.ShapeDtypeStruct((T, PROJ_COLS), BF16),
            jax.ShapeDtypeStruct((T, LANES), F32),
        ],
        scratch_shapes=[pltpu.VMEM((tm, D), BF16)],
        compiler_params=_cparams(2),
        name="ln_inproj",
    )(x2, ln_g, ln_b, w_perm, w_if)


def _alibi_slope(h):
    return float(2.0 ** (-8.0 / N_HEADS * (h + 1)))


def _attn_kernel(sink_ref, q_ref, kp_ref, kc_ref, vp_ref, vc_ref, o_ref, *, blocks_per_seq):
    n = pl.program_id(0) % blocks_per_seq
    W = WINDOW
    qi = lax.broadcasted_iota(jnp.int32, (W, 2 * W), 0)
    kj = lax.broadcasted_iota(jnp.int32, (W, 2 * W), 1)
    dist = qi - kj + W
    valid = (dist >= 0) & (dist < W) & ((kj >= W) | (n > 0))
    neg_dist = jnp.where(valid, -dist.astype(F32), NEG_BIG)

    lane = lax.broadcasted_iota(jnp.int32, (2 * W, LANES), 1)
    lo = lane < HEAD_DIM
    lo_w = lax.broadcasted_iota(jnp.int32, (W, LANES), 1) < HEAD_DIM

    def dup_halves(prev_ref, cur_ref, g):
        a = jnp.concatenate([prev_ref[...], cur_ref[...]], axis=0).astype(F32)
        r = pltpu.roll(a, HEAD_DIM, 1)
        return jnp.where(lo, a, r) if g == 0 else jnp.where(lo, r, a)

    for g in range(N_KV_HEADS):
        kd = dup_halves(kp_ref, kc_ref, g)
        vd = dup_halves(vp_ref, vc_ref, g)
        zero = jnp.zeros_like(kd)
        kbd = jnp.concatenate([jnp.where(lo, kd, zero), jnp.where(lo, zero, kd)], axis=0).astype(BF16)
        vbd = jnp.concatenate([jnp.where(lo, vd, zero), jnp.where(lo, zero, vd)], axis=0).astype(BF16)
        for pp in range(GQA_GROUP // 2):
            pair = g * (GQA_GROUP // 2) + pp
            h_e, h_o = 2 * pair, 2 * pair + 1
            q2 = q_ref[:, pair * LANES:(pair + 1) * LANES]
            s = _dot_nt(q2, kbd)
            outs = []
            for half, h in ((0, h_e), (1, h_o)):
                sh = s[:, half * 2 * W:(half + 1) * 2 * W] + _alibi_slope(h) * neg_dist
                sink = sink_ref[h]
                m = jnp.maximum(jnp.max(sh, axis=1, keepdims=True), sink)
                p = jnp.exp(sh - m)
                l = jnp.sum(p, axis=1, keepdims=True) + jnp.exp(sink - m)
                outs.append((p, l))
            p_all = jnp.concatenate([outs[0][0], outs[1][0]], axis=1).astype(BF16)
            o2 = _dot(p_all, vbd)
            inv = jnp.where(lo_w, 1.0 / outs[0][1], 1.0 / outs[1][1])
            o_ref[:, pair * LANES:(pair + 1) * LANES] = (o2 * inv).astype(o_ref.dtype)


def _attention(proj, sinks, B, S):
    T = B * S
    nb = S // WINDOW
    kcol = C_AK // LANES
    vcol = C_AV // LANES
    kern = functools.partial(_attn_kernel, blocks_per_seq=nb)
    grid_spec = pltpu.PrefetchScalarGridSpec(
        num_scalar_prefetch=1,
        grid=(T // WINDOW,),
        in_specs=[
            pl.BlockSpec((WINDOW, ATTN_Q_W), lambda i, s: (i, C_AQ // ATTN_Q_W)),
            pl.BlockSpec((WINDOW, LANES), lambda i, s: (jnp.maximum(i - 1, 0), kcol)),
            pl.BlockSpec((WINDOW, LANES), lambda i, s: (i, kcol)),
            pl.BlockSpec((WINDOW, LANES), lambda i, s: (jnp.maximum(i - 1, 0), vcol)),
            pl.BlockSpec((WINDOW, LANES), lambda i, s: (i, vcol)),
        ],
        out_specs=pl.BlockSpec((WINDOW, ATTN_Q_W), lambda i, s: (i, 0)),
    )
    return pl.pallas_call(
        kern,
        grid_spec=grid_spec,
        out_shape=jax.ShapeDtypeStruct((T, ATTN_Q_W), BF16),
        compiler_params=_cparams(1),
        name="swa_attention",
    )(sinks, proj, proj, proj, proj, proj)


def _soft_cap(z):
    return GATE_SOFTCAP * jnp.tanh(z / GATE_SOFTCAP)


def _mlstm_kernel(q_ref, k_ref, v_ref, og_ref, gates_ref, gbias_ref, normg_ref, o_ref, c_ref, n_ref, m_ref):
    L = M_CHUNK
    n_seq = q_ref.shape[0]

    @pl.when(pl.program_id(0) == 0)
    def _():
        c_ref[...] = jnp.zeros_like(c_ref)
        n_ref[...] = jnp.zeros_like(n_ref)
        m_ref[...] = jnp.zeros_like(m_ref)

    ti = lax.broadcasted_iota(jnp.int32, (L, L), 0)
    si = lax.broadcasted_iota(jnp.int32, (L, L), 1)
    causal = si <= ti
    tril = causal.astype(F32)

    work = []
    for b in range(n_seq):
        sc = _soft_cap(gates_ref[b] + gbias_ref[...])
        lf = jax.nn.log_sigmoid(sc)
        bcum = jnp.dot(tril, lf, precision=lax.Precision.HIGHEST, preferred_element_type=F32)
        sc_t = sc.T
        bcum_t = bcum.T
        for h in range(M_HEADS):
            st = b * M_HEADS + h
            work.append(dict(
                b=b, h=h, st=st,
                ig_col=sc[:, h:h + 1], ig_row=sc_t[h:h + 1, :],
                b_col=bcum[:, M_HEADS + h:M_HEADS + h + 1], b_row=bcum_t[M_HEADS + h:M_HEADS + h + 1, :],
                m_prev=m_ref[st][:, :1], C=c_ref[st], nvec=n_ref[st],
                qx=q_ref[b, :, h * M_DQK:(h + 1) * M_DQK], kx=k_ref[b, :, h * M_DQK:(h + 1) * M_DQK],
                vx=v_ref[b, :, h * M_DV:(h + 1) * M_DV], og=og_ref[b, :, h * M_DV:(h + 1) * M_DV],
                normg=normg_ref[h]))

    results = []
    for wk in work:
        ig_col, ig_row, b_col, b_row = wk["ig_col"], wk["ig_row"], wk["b_col"], wk["b_row"]
        m_prev, C, nvec, qx, kx, vx = wk["m_prev"], wk["C"], wk["nvec"], wk["qx"], wk["kx"], wk["vx"]

        inter = b_col + m_prev
        dmat = jnp.where(causal, b_col - b_row + ig_row, NEG_BIG)
        mt = jnp.maximum(inter, jnp.max(dmat, axis=1, keepdims=True))
        w = jnp.exp(dmat - mt)
        sqk = _dot_nt(qx, kx) * w
        s_i = jnp.exp(inter - mt)
        num = s_i * _dot(qx, C.astype(BF16)) + _dot(sqk.astype(BF16), vx)
        nq = s_i * jnp.sum(qx.astype(F32) * nvec, axis=1, keepdims=True) + jnp.sum(sqk, axis=1, keepdims=True)
        hh = num / jnp.maximum(jnp.abs(nq), jnp.exp(-mt))

        bl = b_col[L - 1:L, :]
        g_col = bl - b_col + ig_col
        m_new = jnp.maximum(bl + m_prev, jnp.max(g_col, axis=0, keepdims=True))
        decay = jnp.exp(bl + m_prev - m_new)
        wg = jnp.exp(g_col - m_new)
        kw = kx.astype(F32) * wg
        c_new = decay * C + _dot(kw.T.astype(BF16), vx)
        n_new = decay * nvec + jnp.sum(kw, axis=0, keepdims=True)

        hn = hh * lax.rsqrt(jnp.mean(hh * hh, axis=1, keepdims=True) + RMS_EPS) * wk["normg"]
        out = (hn * jax.nn.sigmoid(wk["og"].astype(F32))).astype(o_ref.dtype)
        results.append((c_new, n_new, m_new, out))

    for wk, (c_new, n_new, m_new, out) in zip(work, results):
        st, b, h = wk["st"], wk["b"], wk["h"]
        c_ref[st] = c_new
        n_ref[st] = n_new
        m_ref[st] = jnp.broadcast_to(m_new, (1, LANES))
        o_ref[b, :, h * M_DV:(h + 1) * M_DV] = out


def _mlstm(proj, gates, gbias, norm_g, B, S):
    nc = S // M_CHUNK
    L = M_CHUNK
    proj3 = proj.reshape(B, S, PROJ_COLS)
    gates3 = gates.reshape(B, S, LANES)
    out = pl.pallas_call(
        _mlstm_kernel,
        grid=(nc,),
        in_specs=[
            pl.BlockSpec((B, L, M_QK_W), lambda c: (0, c, C_MQ // M_QK_W)),
            pl.BlockSpec((B, L, M_QK_W), lambda c: (0, c, C_MK // M_QK_W)),
            pl.BlockSpec((B, L, M_V_W), lambda c: (0, c, C_MV // M_V_W)),
            pl.BlockSpec((B, L, M_V_W), lambda c: (0, c, C_MO // M_V_W)),
            pl.BlockSpec((B, L, LANES), lambda c: (0, c, 0)),
            pl.BlockSpec((1, LANES), lambda c: (0, 0)),
            pl.BlockSpec((M_HEADS, 1, M_DV), lambda c: (0, 0, 0)),
        ],
        out_specs=pl.BlockSpec((B, L, M_V_W), lambda c: (0, c, 0)),
        out_shape=jax.ShapeDtypeStruct((B, S, M_V_W), BF16),
        scratch_shapes=[
            pltpu.VMEM((B * M_HEADS, M_DQK, M_DV), F32),
            pltpu.VMEM((B * M_HEADS, 1, M_DQK), F32),
            pltpu.VMEM((B * M_HEADS, 1, LANES), F32),
        ],
        compiler_params=_cparams(1),
        name="mlstm_chunkwise",
    )(proj3, proj3, proj3, proj3, gates3, gbias, norm_g)
    return out.reshape(B * S, M_V_W)


def _mix_kernel(attn_ref, mh_ref, ga_ref, gb_ref, x_ref, lng_ref, lnb_ref, wba_ref, wbm_ref, wout_ref,
                mg_ref, mb_ref, o_ref):
    a = _dot(attn_ref[...], wba_ref[...])
    m = _dot(mh_ref[...], wbm_ref[...])
    merged = jax.nn.sigmoid(ga_ref[...].astype(F32)) * a + jax.nn.sigmoid(gb_ref[...].astype(F32)) * m
    y = _dot(merged.astype(BF16), wout_ref[...])
    h0 = _layer_norm(x_ref[...], lng_ref[...], lnb_ref[...])
    x1 = _layer_norm(DEEPNORM_ALPHA * h0 + y, mg_ref[...], mb_ref[...])
    _store_rows(o_ref, x1)


def _mix(attn, mh, proj, x2, ln_g, ln_b, w_ba, w_bm, w_out, mix_g, mix_b, tm):
    T, D = x2.shape
    return pl.pallas_call(
        _mix_kernel,
        grid=(T // tm,),
        in_specs=[
            pl.BlockSpec((tm, ATTN_Q_W), lambda i: (i, 0)),
            pl.BlockSpec((tm, M_V_W), lambda i: (i, 0)),
            pl.BlockSpec((tm, D), lambda i: (i, C_GA // D_MODEL)),
            pl.BlockSpec((tm, D), lambda i: (i, C_GB // D_MODEL)),
            pl.BlockSpec((tm, D), lambda i: (i, 0)),
            _const_spec((1, D)), _const_spec((1, D)),
            _const_spec((ATTN_Q_W, D)), _const_spec((M_V_W, D)), _const_spec((D, D)),
            _const_spec((1, D)), _const_spec((1, D)),
        ],
        out_specs=pl.BlockSpec(_row_shape(tm), lambda i: (0, i, 0)),
        out_shape=jax.ShapeDtypeStruct(_row_shape(T), F32),
        compiler_params=_cparams(1),
        name="mix_outproj_ln",
    )(attn, mh, proj, proj, x2, ln_g, ln_b, w_ba, w_bm, w_out, mix_g, mix_b)


def _router_kernel(x_ref, wa_ref, wb_ref, br_ref, idx_ref, gw_ref, rank_ref, cnt_ref, carry_ref):
    i = pl.program_id(0)
    tm = x_ref.shape[1] // SUBLANES
    E = N_EXPERTS

    @pl.when(i == 0)
    def _():
        carry_ref[...] = jnp.zeros_like(carry_ref)

    x = _load_rows(x_ref)
    xh = x.astype(BF16)
    xl = (x - xh.astype(F32)).astype(BF16)
    c = _dot(xh, wa_ref[...]) + _dot(xl, wb_ref[...])
    logits_t = (c + pltpu.roll(c, E, 1)).T[:E]
    scores = jax.nn.sigmoid(logits_t)
    sel = scores + br_ref[...]
    erow = lax.broadcasted_iota(jnp.int32, (E, tm), 0).astype(F32)

    idx_rows, g_rows, hits = [], [], []
    for _ in range(TOP_K):
        mx = jnp.max(sel, axis=0, keepdims=True)
        am = jnp.min(jnp.where(sel == mx, erow, float(E)), axis=0, keepdims=True)
        hit = erow == am
        g_rows.append(jnp.sum(jnp.where(hit, scores, 0.0), axis=0, keepdims=True))
        sel = jnp.where(hit, -jnp.inf, sel)
        idx_rows.append(am)
        hits.append(hit)
    gsum = functools.reduce(lambda a, b: a + b, g_rows)
    chosen = functools.reduce(lambda a, b: a + b, [h.astype(F32) for h in hits])

    r_i = lax.broadcasted_iota(jnp.int32, (tm, tm), 0)
    c_i = lax.broadcasted_iota(jnp.int32, (tm, tm), 1)
    before = (r_i < c_i).astype(BF16)
    prefix = _dot(chosen.astype(BF16), before) + carry_ref[:, :1]
    rank_rows = [jnp.sum(jnp.where(h, prefix, 0.0), axis=0, keepdims=True) for h in hits]

    krow = lax.broadcasted_iota(jnp.int32, (TOP_K, tm), 0)

    def stack(rows):
        out = jnp.zeros((TOP_K, tm), F32)
        for k, r in enumerate(rows):
            out = jnp.where(krow == k, r, out)
        return out

    idx_ref[...] = stack(idx_rows).astype(jnp.int32)
    gw_ref[...] = stack(g_rows) / gsum * ROUTED_SCALE
    rank_ref[...] = stack(rank_rows).astype(jnp.int32)
    carry_ref[...] = carry_ref[...] + jnp.sum(chosen, axis=1, keepdims=True)
    cnt_ref[...] = carry_ref[...]


def _router(x1, w_a, w_b, b_col, tm):
    T, D = x1.shape[1] // SUBLANES, D_MODEL
    tok = lambda i: (0, i)
    return pl.pallas_call(
        _router_kernel,
        grid=(T // tm,),
        in_specs=[
            pl.BlockSpec(_row_shape(tm), lambda i: (0, i, 0)),
            _const_spec((D, 2 * N_EXPERTS)),
            _const_spec((D, 2 * N_EXPERTS)),
            _const_spec((N_EXPERTS, 1)),
        ],
        out_specs=[pl.BlockSpec((TOP_K, tm), tok), pl.BlockSpec((TOP_K, tm), tok), pl.BlockSpec((TOP_K, tm), tok),
                   pl.BlockSpec((N_EXPERTS, LANES), lambda i: (0, 0))],
        out_shape=[jax.ShapeDtypeStruct((TOP_K, T), jnp.int32), jax.ShapeDtypeStruct((TOP_K, T), F32),
                   jax.ShapeDtypeStruct((TOP_K, T), jnp.int32), jax.ShapeDtypeStruct((N_EXPERTS, LANES), F32)],
        scratch_shapes=[pltpu.VMEM((N_EXPERTS, LANES), F32)],
        compiler_params=_cparams(1),
        name="router_topk_rank",
    )(x1, w_a, w_b, b_col)


ZFILL_SIZES = tuple(2 ** b for b in reversed(range(MOE_BLOCK.bit_length() - 1)))
assert MOE_BLOCK & (MOE_BLOCK - 1) == 0


def _dispatch_kernel(zrow_ref, dest_hbm, xp_ref, xs_hbm, idx_smem, zbuf, isem, zsem, rsem):
    i = pl.program_id(0)
    nsteps = pl.num_programs(0)
    tm = xp_ref.shape[1] // SUBLANES
    n = tm * TOP_K

    def idx_copy(tile, slot):
        return pltpu.make_async_copy(dest_hbm.at[pl.ds(tile * n, n)],
                                     idx_smem.at[pl.ds(pl.multiple_of(slot * n, n), n)], isem.at[slot])

    def zero_copy(row, rows):
        return pltpu.make_async_copy(zbuf.at[:, pl.ds(0, rows * SUBLANES)], xs_hbm.at[:, _tile_rows(row, rows)], zsem)

    def zero_fill_plan():
        plan = []
        for e in range(N_EXPERTS):
            start = zrow_ref[e]
            rem = zrow_ref[N_EXPERTS + e] - start
            for rows in ZFILL_SIZES:
                take = rem >= rows
                plan.append((take, start, rows))
                start = jnp.where(take, start + rows, start)
                rem = jnp.where(take, rem - rows, rem)
        n_alloc = xs_hbm.shape[1] // SUBLANES
        for j in range(N_EXPERTS + 2):
            s = zrow_ref[2 * N_EXPERTS] + j * MOE_BLOCK
            plan.append((s < n_alloc, jnp.minimum(s, n_alloc - MOE_BLOCK), MOE_BLOCK))
        return plan

    @pl.when(i == 0)
    def _():
        idx_copy(0, 0).start()
        zbuf[...] = jnp.zeros_like(zbuf)
        plan = zero_fill_plan()
        for take, start, rows in plan:
            @pl.when(take)
            def _():
                zero_copy(start, rows).start()
        for take, start, rows in plan:
            @pl.when(take)
            def _():
                zero_copy(0, rows).wait()

    slot = i % 2
    idx_copy(i, slot).wait()

    @pl.when(i + 1 < nsteps)
    def _():
        idx_copy(i + 1, 1 - slot).start()

    base = slot * n

    def body(t, carry):
        for k in range(TOP_K):
            d = idx_smem[base + t * TOP_K + k]
            pltpu.make_async_copy(xp_ref.at[:, _tile_rows(t)], xs_hbm.at[:, _tile_rows(d)],
                                  rsem).start(priority=k % 2)
        return carry

    lax.fori_loop(0, tm, body, 0)
    for _ in range(TOP_K):
        pltpu.make_async_copy(xp_ref, xs_hbm.at[:, _tile_rows(0, tm)], rsem).wait()


def _dispatch(x1p, dest_flat, zrow, n_rows_alloc, tm):
    T = x1p.shape[1] // SUBLANES
    grid_spec = pltpu.PrefetchScalarGridSpec(
        num_scalar_prefetch=1,
        grid=(T // tm,),
        in_specs=[
            pl.BlockSpec(memory_space=pl.ANY),
            pl.BlockSpec(_row_shape(tm), lambda i, z: (0, i, 0)),
        ],
        out_specs=pl.BlockSpec(memory_space=pl.ANY),
        scratch_shapes=[
            pltpu.SMEM((2 * tm * TOP_K,), jnp.int32),
            pltpu.VMEM(_row_shape(MOE_BLOCK), F32),
            pltpu.SemaphoreType.DMA((2,)),
            pltpu.SemaphoreType.DMA(()),
            pltpu.SemaphoreType.DMA(()),
        ],
    )
    return pl.pallas_call(
        _dispatch_kernel,
        grid_spec=grid_spec,
        out_shape=jax.ShapeDtypeStruct(_row_shape(n_rows_alloc), F32),
        compiler_params=_cparams(1),
        name="moe_dispatch",
    )(zrow, dest_flat, x1p)


def _expert_kernel(be_ref, nused_ref, xs_ref, wg_ref, wu_ref, wd_ref, y_ref, wg_s, wu_s, wd_s):
    i = pl.program_id(0)
    used = i < nused_ref[0]
    new_expert = (i == 0) | (be_ref[i] != be_ref[jnp.maximum(i - 1, 0)])

    @pl.when(used & new_expert)
    def _():
        wg_s[...] = wg_ref[0].astype(BF16)
        wu_s[...] = wu_ref[0].astype(BF16)
        wd_s[...] = wd_ref[0].astype(BF16)

    @pl.when(used)
    def _():
        xb = _load_rows(xs_ref).astype(BF16)
        hb = jax.nn.silu(_dot(xb, wg_s[...])) * _dot(xb, wu_s[...])
        y = _dot(hb.astype(BF16), wd_s[...])
        _store_rows(y_ref, y)

    @pl.when(jnp.logical_not(used))
    def _():
        y_ref[...] = jnp.zeros_like(y_ref)


def _experts(xs, block_e, n_used, w_eg, w_eu, w_ed, n_blocks):
    D = D_MODEL
    last_used = lambda i, nu: jnp.minimum(i, nu[0] - 1)
    row_block = _row_shape(MOE_BLOCK)
    grid_spec = pltpu.PrefetchScalarGridSpec(
        num_scalar_prefetch=2,
        grid=(n_blocks,),
        in_specs=[
            pl.BlockSpec(row_block, lambda i, be, nu: (0, last_used(i, nu), 0)),
            pl.BlockSpec((1, D, D_EXPERT), lambda i, be, nu: (be[i], 0, 0)),
            pl.BlockSpec((1, D, D_EXPERT), lambda i, be, nu: (be[i], 0, 0)),
            pl.BlockSpec((1, D_EXPERT, D), lambda i, be, nu: (be[i], 0, 0)),
        ],
        out_specs=pl.BlockSpec(row_block, lambda i, be, nu: (0, i, 0)),
        scratch_shapes=[pltpu.VMEM((D, D_EXPERT), BF16), pltpu.VMEM((D, D_EXPERT), BF16),
                        pltpu.VMEM((D_EXPERT, D), BF16)],
    )
    return pl.pallas_call(
        _expert_kernel,
        grid_spec=grid_spec,
        out_shape=jax.ShapeDtypeStruct(_row_shape(n_blocks * MOE_BLOCK), F32),
        compiler_params=_cparams(1),
        name="expert_mlp",
    )(block_e, n_used, xs, w_eg, w_eu, w_ed)


def _combine_kernel(dest_hbm, y_hbm, gw_ref, o_ref, idx_smem, ybuf, isem, rsem):
    i = pl.program_id(0)
    nsteps = pl.num_programs(0)
    tm = o_ref.shape[0]
    n = tm * TOP_K

    def idx_copy(tile, slot):
        return pltpu.make_async_copy(dest_hbm.at[pl.ds(tile * n, n)],
                                     idx_smem.at[pl.ds(pl.multiple_of(slot * n, n), n)], isem.at[slot])

    def start_row_gathers(slot):
        base = slot * n

        def body(t, carry):
            for k in range(TOP_K):
                d = idx_smem[base + t * TOP_K + k]
                pltpu.make_async_copy(y_hbm.at[:, _tile_rows(d)], ybuf.at[slot, k, :, _tile_rows(t)],
                                      rsem.at[slot]).start(priority=k % 2)
            return carry

        lax.fori_loop(0, tm, body, 0)

    slot = i % 2
    nxt = 1 - slot

    @pl.when(i == 0)
    def _():
        idx_copy(0, 0).start()
        idx_copy(0, 0).wait()
        start_row_gathers(0)

        @pl.when(nsteps > 1)
        def _():
            idx_copy(1, 1).start()

    @pl.when(i + 1 < nsteps)
    def _():
        idx_copy(i + 1, nxt).wait()

        @pl.when(i + 2 < nsteps)
        def _():
            idx_copy(i + 2, slot).start()

        start_row_gathers(nxt)

    for k in range(TOP_K):
        pltpu.make_async_copy(y_hbm.at[:, _tile_rows(0, tm)], ybuf.at[slot, k], rsem.at[slot]).wait()
    gw = gw_ref[...]
    routed = jnp.zeros((tm, D_MODEL), F32)
    for k in range(TOP_K):
        routed = routed + gw[:, k:k + 1] * _load_rows(ybuf.at[slot, k])
    o_ref[...] = routed


def _combine(dest_flat, y, gw_tok, tm):
    T = gw_tok.shape[0]
    return pl.pallas_call(
        _combine_kernel,
        grid=(T // tm,),
        in_specs=[
            pl.BlockSpec(memory_space=pl.ANY),
            pl.BlockSpec(memory_space=pl.ANY),
            pl.BlockSpec((tm, TOP_K), lambda i: (i, 0)),
        ],
        out_specs=pl.BlockSpec((tm, D_MODEL), lambda i: (i, 0)),
        out_shape=jax.ShapeDtypeStruct((T, D_MODEL), F32),
        scratch_shapes=[
            pltpu.SMEM((2 * tm * TOP_K,), jnp.int32),
            pltpu.VMEM((2, TOP_K) + _row_shape(tm), F32),
            pltpu.SemaphoreType.DMA((2,)),
            pltpu.SemaphoreType.DMA((2,)),
        ],
        compiler_params=_cparams(1),
        name="moe_combine",
    )(dest_flat, y, gw_tok)


def _ffn_ple_kernel(x1_ref, yr_ref, p_ref, wsg_ref, wsu_ref, wsd_ref, fg_ref, fb_ref, wpp_ref, wpg_ref,
                    pg_ref, pb_ref, o_ref):
    x1 = _load_rows(x1_ref)
    xb = x1.astype(BF16)
    hs = jax.nn.silu(_dot(xb, wsg_ref[...])) * _dot(xb, wsu_ref[...])
    shared = _dot(hs.astype(BF16), wsd_ref[...])
    x2 = _layer_norm(DEEPNORM_ALPHA * x1 + (yr_ref[...] + shared), fg_ref[...], fb_ref[...])
    ple = _dot(p_ref[...].astype(BF16), wpp_ref[...]) * jax.nn.sigmoid(_dot(x2.astype(BF16), wpg_ref[...]))
    o_ref[...] = _layer_norm(DEEPNORM_ALPHA * x2 + ple, pg_ref[...], pb_ref[...])


def _ffn_ple(x1, y_routed, p2, w_sg, w_su, w_sd, ffn_g, ffn_b, w_pp, w_pg, ple_g, ple_b, tm):
    T, D = y_routed.shape
    return pl.pallas_call(
        _ffn_ple_kernel,
        grid=(T // tm,),
        in_specs=[
            pl.BlockSpec(_row_shape(tm), lambda i: (0, i, 0)),
            pl.BlockSpec((tm, D), lambda i: (i, 0)),
            pl.BlockSpec((tm, PLE_DIM), lambda i: (i, 0)),
            _const_spec((D, D_SHARED)), _const_spec((D, D_SHARED)), _const_spec((D_SHARED, D)),
            _const_spec((1, D)), _const_spec((1, D)),
            _const_spec((PLE_DIM, D)), _const_spec((D, D)),
            _const_spec((1, D)), _const_spec((1, D)),
        ],
        out_specs=pl.BlockSpec((tm, D), lambda i: (i, 0)),
        out_shape=jax.ShapeDtypeStruct((T, D), F32),
        compiler_params=_cparams(1),
        name="shared_ffn_ple_ln",
    )(x1, y_routed, p2, w_sg, w_su, w_sd, ffn_g, ffn_b, w_pp, w_pg, ple_g, ple_b)


def _permute_w_in(w_in):
    D = w_in.shape[0]
    sizes = (ATTN_Q_W, ATTN_KV_W, ATTN_KV_W, M_QK_W, M_QK_W, M_V_W, M_V_W, M_HEADS, M_HEADS, D_MODEL, D_MODEL)
    offs = np.concatenate([[0], np.cumsum(sizes)])
    seg = [w_in[:, offs[k]:offs[k + 1]] for k in range(len(sizes))]
    aq, ak, av, mq, mk, mv, mo, mi, mf, ga, gb = seg
    aq = aq * (HEAD_DIM ** -0.5)
    mk = mk * (M_DQK ** -0.5)
    gates = jnp.concatenate([mi, mf, jnp.zeros((D, LANES - 2 * M_HEADS), w_in.dtype)], axis=1)
    return jnp.concatenate([ga, gb, aq, mv, mo, mq, mk, ak, av], axis=1).astype(BF16), gates.astype(BF16)


def _route_layout(idx_t, rank_t, counts, T):
    n_pairs = T * TOP_K
    n_blocks = -(-n_pairs // MOE_BLOCK) + N_EXPERTS
    padded = (counts + MOE_BLOCK - 1) // MOE_BLOCK * MOE_BLOCK
    ends = jnp.cumsum(padded)
    pstarts = ends - padded
    onehot = idx_t[:, :, None] == jnp.arange(N_EXPERTS, dtype=jnp.int32)
    dest_t = jnp.sum(jnp.where(onehot, pstarts.astype(jnp.int32), 0), axis=-1) + rank_t
    dest_flat = dest_t.T.reshape(-1)
    block_row = jnp.arange(n_blocks, dtype=jnp.int32) * MOE_BLOCK
    block_e = jnp.minimum(jnp.sum(ends[None, :] <= block_row[:, None], axis=1), N_EXPERTS - 1).astype(jnp.int32)
    n_used = (ends[-1] // MOE_BLOCK).astype(jnp.int32).reshape(1)
    zrow = jnp.concatenate([pstarts + counts, ends, ends[-1:]]).astype(jnp.int32)
    return dest_flat, block_e, n_used, zrow, n_blocks


def kernel(x, p, ln_in_g, ln_in_b, w_in, attn_sinks, mlstm_b_i, mlstm_b_f, mlstm_norm_g, w_branch_attn, w_branch_mlstm, w_out, ln_mix_g, ln_mix_b, w_router, b_router, w_exp_gate, w_exp_up, w_exp_down, w_sh_gate, w_sh_up, w_sh_down, ln_ffn_g, ln_ffn_b, w_ple_proj, w_ple_gate, ln_ple_g, ln_ple_b):
    B, S, D = x.shape
    T = B * S
    assert D == D_MODEL and S % WINDOW == 0 and S % M_CHUNK == 0 and w_in.shape[0] == DEPTH
    x2 = x.reshape(T, D)
    row = lambda v: v.reshape(1, -1).astype(F32)

    tm_proj = min(1024, T)
    tm_mix = min(256, T)
    tm_router = min(512, T)
    tm_dispatch = min(512, T)
    tm_combine = min(256, T)
    tm_ffn = min(256, T)

    w_perm, w_if = _permute_w_in(w_in[0])
    proj, gates = _in_projection(x2, row(ln_in_g), row(ln_in_b), w_perm, w_if, tm_proj)

    attn = _attention(proj, attn_sinks[0].astype(F32), B, S)

    gbias = jnp.concatenate([mlstm_b_i[0], mlstm_b_f[0], jnp.zeros((LANES - 2 * M_HEADS,), F32)]).reshape(1, LANES)
    mh = _mlstm(proj, gates, gbias, mlstm_norm_g[0].reshape(M_HEADS, 1, M_DV).astype(F32), B, S)

    x1 = _mix(attn, mh, proj, x2, row(ln_in_g), row(ln_in_b),
              w_branch_attn[0].astype(BF16), w_branch_mlstm[0].astype(BF16), w_out[0].astype(BF16),
              row(ln_mix_g[0]), row(ln_mix_b[0]), tm_mix)

    wr = w_router[0].astype(F32)
    wr_hi = wr.astype(BF16)
    wr_lo = (wr - wr_hi.astype(F32)).astype(BF16)
    w_a = jnp.concatenate([wr_hi, wr_lo], axis=1)
    w_b = jnp.concatenate([jnp.zeros_like(wr_hi), wr_hi], axis=1)
    idx_t, gw_t, rank_t, counts_f = _router(x1, w_a, w_b, b_router[0].reshape(N_EXPERTS, 1).astype(F32), tm_router)

    counts = counts_f[:, 0].astype(jnp.int32)
    dest_flat, block_e, n_used, zrow, n_blocks = _route_layout(idx_t, rank_t, counts, T)
    xs = _dispatch(x1, dest_flat, zrow, n_blocks * MOE_BLOCK, tm_dispatch)
    y = _experts(xs, block_e, n_used, w_exp_gate[0], w_exp_up[0], w_exp_down[0], n_blocks)

    y_routed = _combine(dest_flat, y, gw_t.T, tm_combine)
    out = _ffn_ple(x1, y_routed, p[0].reshape(T, PLE_DIM),
                   w_sh_gate[0].astype(BF16), w_sh_up[0].astype(BF16), w_sh_down[0].astype(BF16),
                   row(ln_ffn_g[0]), row(ln_ffn_b[0]),
                   w_ple_proj[0].astype(BF16), w_ple_gate[0].astype(BF16),
                   row(ln_ple_g[0]), row(ln_ple_b[0]), tm_ffn)
    return out.reshape(B, S, D)
```

```python
import functools

import numpy as np
import jax
import jax.numpy as jnp
from jax import lax
from jax.experimental import pallas as pl
from jax.experimental.pallas import tpu as pltpu

F32 = jnp.float32
BF16 = jnp.bfloat16

D_MODEL = 2048
PLE_DIM = 256
HEAD_DIM = 64
N_HEADS = 16
N_KV_HEADS = 2
GQA_GROUP = N_HEADS // N_KV_HEADS
WINDOW = 128
M_HEADS = 4
M_DV = 256
M_DQK = 128
M_CHUNK = 256
GATE_SOFTCAP = 15.0
N_EXPERTS = 64
TOP_K = 8
D_EXPERT = 512
D_SHARED = 512
ROUTED_SCALE = 2.5
MOE_BLOCK = 512
LN_EPS = 1e-5
RMS_EPS = 1e-6
DEPTH = 1
DEEPNORM_ALPHA = (2.0 * DEPTH) ** 0.25

ATTN_Q_W = N_HEADS * HEAD_DIM
ATTN_KV_W = N_KV_HEADS * HEAD_DIM
M_QK_W = M_HEADS * M_DQK
M_V_W = M_HEADS * M_DV

LANES = 128
NEG_BIG = -1e30

C_GA = 0
C_GB = C_GA + D_MODEL
C_AQ = C_GB + D_MODEL
C_MV = C_AQ + ATTN_Q_W
C_MO = C_MV + M_V_W
C_MQ = C_MO + M_V_W
C_MK = C_MQ + M_QK_W
C_AK = C_MK + M_QK_W
C_AV = C_AK + ATTN_KV_W
PROJ_COLS = C_AV + ATTN_KV_W
PROJ_TN = 1408
assert PROJ_COLS % PROJ_TN == 0 and PROJ_TN % LANES == 0

SUBLANES = 8
ROW_PLANES = D_MODEL // (SUBLANES * LANES)
assert ROW_PLANES * SUBLANES * LANES == D_MODEL

VMEM_LIMIT = 56 * 1024 * 1024


def _cparams(n_axes):
    return pltpu.CompilerParams(dimension_semantics=("arbitrary",) * n_axes, vmem_limit_bytes=VMEM_LIMIT)


def _const_spec(shape):
    nd = len(shape)
    return pl.BlockSpec(shape, lambda *_: (0,) * nd, pipeline_mode=pl.Buffered(1))


def _layer_norm(x, g, b):
    mu = jnp.mean(x, axis=-1, keepdims=True)
    xc = x - mu
    var = jnp.mean(xc * xc, axis=-1, keepdims=True)
    return xc * lax.rsqrt(var + LN_EPS) * g + b


def _dot(a, b):
    return jnp.dot(a, b, preferred_element_type=F32)


def _dot_nt(a, b):
    return lax.dot_general(a, b, (((1,), (1,)), ((), ())), preferred_element_type=F32)


def _row_shape(rows):
    return (ROW_PLANES, rows * SUBLANES, LANES)


def _store_rows(ref, val):
    rows = val.shape[0]
    for h in range(ROW_PLANES):
        for c in range(SUBLANES):
            col = (h * SUBLANES + c) * LANES
            ref[h, pl.ds(c, rows, stride=SUBLANES), :] = val[:, col:col + LANES]


def _load_rows(ref):
    rows = ref.shape[1] // SUBLANES
    return jnp.concatenate([ref[h, pl.ds(c, rows, stride=SUBLANES), :]
                            for h in range(ROW_PLANES) for c in range(SUBLANES)], axis=1)


def _tile_rows(r, rows=1):
    return pl.ds(pl.multiple_of(r * SUBLANES, SUBLANES), rows * SUBLANES)


def _inproj_kernel(x_ref, g_ref, b_ref, w_ref, wif_ref, o_ref, og_ref, xn_ref):
    j = pl.program_id(1)

    @pl.when(j == 0)
    def _():
        xn = _layer_norm(x_ref[...], g_ref[...], b_ref[...]).astype(BF16)
        xn_ref[...] = xn
        og_ref[...] = _dot(xn, wif_ref[...])

    o_ref[...] = _dot(xn_ref[...], w_ref[...]).astype(o_ref.dtype)


def _in_projection(x2, ln_g, ln_b, w_perm, w_if, tm):
    T, D = x2.shape
    return pl.pallas_call(
        _inproj_kernel,
        grid=(T // tm, PROJ_COLS // PROJ_TN),
        in_specs=[
            pl.BlockSpec((tm, D), lambda i, j: (i, 0)),
            pl.BlockSpec((1, D), lambda i, j: (0, 0)),
            pl.BlockSpec((1, D), lambda i, j: (0, 0)),
            pl.BlockSpec((D, PROJ_TN), lambda i, j: (0, j)),
            pl.BlockSpec((D, LANES), lambda i, j: (0, 0)),
        ],
        out_specs=[
            pl.BlockSpec((tm, PROJ_TN), lambda i, j: (i, j)),
            pl.BlockSpec((tm, LANES), lambda i, j: (i, 0)),
        ],
        out_shape=[
            jax.ShapeDtypeStruct((T, PROJ_COLS), BF16),
            jax.ShapeDtypeStruct((T, LANES), F32),
        ],
        scratch_shapes=[pltpu.VMEM((tm, D), BF16)],
        compiler_params=_cparams(2),
        name="ln_inproj",
    )(x2, ln_g, ln_b, w_perm, w_if)


def _alibi_slope(h):
    return float(2.0 ** (-8.0 / N_HEADS * (h + 1)))


def _attn_kernel(sink_ref, q_ref, kp_ref, kc_ref, vp_ref, vc_ref, o_ref, s_ref, p_ref, inv_ref, *, blocks_per_seq):
    n = pl.program_id(0) % blocks_per_seq
    W = WINDOW
    qi = lax.broadcasted_iota(jnp.int32, (W, 2 * W), 0)
    kj = lax.broadcasted_iota(jnp.int32, (W, 2 * W), 1)
    dist = qi - kj + W
    valid = (dist >= 0) & (dist < W) & ((kj >= W) | (n > 0))
    neg_dist = jnp.where(valid, -dist.astype(F32), NEG_BIG)

    lane = lax.broadcasted_iota(jnp.int32, (2 * W, LANES), 1)
    lo = lane < HEAD_DIM

    def dup_halves(prev_ref, cur_ref, g):
        a = jnp.concatenate([prev_ref[...], cur_ref[...]], axis=0).astype(F32)
        r = pltpu.roll(a, HEAD_DIM, 1)
        return jnp.where(lo, a, r) if g == 0 else jnp.where(lo, r, a)

    n_pairs = N_HEADS // 2
    vbds = []
    for g in range(N_KV_HEADS):
        kd = dup_halves(kp_ref, kc_ref, g)
        vd = dup_halves(vp_ref, vc_ref, g)
        zero = jnp.zeros_like(kd)
        kbd = jnp.concatenate([jnp.where(lo, kd, zero), jnp.where(lo, zero, kd)], axis=0).astype(BF16)
        vbds.append(jnp.concatenate([jnp.where(lo, vd, zero), jnp.where(lo, zero, vd)], axis=0).astype(BF16))
        for pp in range(GQA_GROUP // 2):
            pair = g * (GQA_GROUP // 2) + pp
            q2 = q_ref[:, pair * LANES:(pair + 1) * LANES]
            s_ref[pair] = _dot_nt(q2, kbd)

    for pair in range(n_pairs):
        s = s_ref[pair]
        ps = []
        for half, h in ((0, 2 * pair), (1, 2 * pair + 1)):
            sh = s[:, half * 2 * W:(half + 1) * 2 * W] + _alibi_slope(h) * neg_dist
            sink = sink_ref[h]
            m = jnp.maximum(jnp.max(sh, axis=1, keepdims=True), sink)
            p = jnp.exp(sh - m)
            l = jnp.sum(p, axis=1, keepdims=True) + jnp.exp(sink - m)
            ps.append(p)
            inv_ref[pair, :, half * HEAD_DIM:(half + 1) * HEAD_DIM] = jnp.broadcast_to(1.0 / l, (W, HEAD_DIM))
        p_ref[pair] = jnp.concatenate(ps, axis=1).astype(BF16)

    for pair in range(n_pairs):
        o2 = _dot(p_ref[pair], vbds[pair // (GQA_GROUP // 2)])
        o_ref[:, pair * LANES:(pair + 1) * LANES] = (o2 * inv_ref[pair]).astype(o_ref.dtype)


def _attention(proj, sinks, B, S):
    T = B * S
    nb = S // WINDOW
    kcol = C_AK // LANES
    vcol = C_AV // LANES
    kern = functools.partial(_attn_kernel, blocks_per_seq=nb)
    grid_spec = pltpu.PrefetchScalarGridSpec(
        num_scalar_prefetch=1,
        grid=(T // WINDOW,),
        in_specs=[
            pl.BlockSpec((WINDOW, ATTN_Q_W), lambda i, s: (i, C_AQ // ATTN_Q_W)),
            pl.BlockSpec((WINDOW, LANES), lambda i, s: (jnp.maximum(i - 1, 0), kcol)),
            pl.BlockSpec((WINDOW, LANES), lambda i, s: (i, kcol)),
            pl.BlockSpec((WINDOW, LANES), lambda i, s: (jnp.maximum(i - 1, 0), vcol)),
            pl.BlockSpec((WINDOW, LANES), lambda i, s: (i, vcol)),
        ],
        out_specs=pl.BlockSpec((WINDOW, ATTN_Q_W), lambda i, s: (i, 0)),
        scratch_shapes=[
            pltpu.VMEM((N_HEADS // 2, WINDOW, 4 * WINDOW), F32),
            pltpu.VMEM((N_HEADS // 2, WINDOW, 4 * WINDOW), BF16),
            pltpu.VMEM((N_HEADS // 2, WINDOW, LANES), F32),
        ],
    )
    return pl.pallas_call(
        kern,
        grid_spec=grid_spec,
        out_shape=jax.ShapeDtypeStruct((T, ATTN_Q_W), BF16),
        compiler_params=_cparams(1),
        name="swa_attention",
    )(sinks, proj, proj, proj, proj, proj)


def _soft_cap(z):
    return GATE_SOFTCAP * jnp.tanh(z / GATE_SOFTCAP)


def _mlstm_kernel(q_ref, k_ref, v_ref, og_ref, gates_ref, gbias_ref, normg_ref, o_ref, c_ref, n_ref, m_ref):
    L = M_CHUNK
    n_seq = q_ref.shape[0]

    @pl.when(pl.program_id(0) == 0)
    def _():
        c_ref[...] = jnp.zeros_like(c_ref)
        n_ref[...] = jnp.zeros_like(n_ref)
        m_ref[...] = jnp.zeros_like(m_ref)

    ti = lax.broadcasted_iota(jnp.int32, (L, L), 0)
    si = lax.broadcasted_iota(jnp.int32, (L, L), 1)
    causal = si <= ti
    tril = causal.astype(F32)

    work = []
    for b in range(n_seq):
        sc = _soft_cap(gates_ref[b] + gbias_ref[...])
        lf = jax.nn.log_sigmoid(sc)
        bcum = jnp.dot(tril, lf, precision=lax.Precision.HIGHEST, preferred_element_type=F32)
        sc_t = sc.T
        bcum_t = bcum.T
        for h in range(M_HEADS):
            st = b * M_HEADS + h
            work.append(dict(
                b=b, h=h, st=st,
                ig_col=sc[:, h:h + 1], ig_row=sc_t[h:h + 1, :],
                b_col=bcum[:, M_HEADS + h:M_HEADS + h + 1], b_row=bcum_t[M_HEADS + h:M_HEADS + h + 1, :],
                m_prev=m_ref[st][:, :1], C=c_ref[st], nvec=n_ref[st],
                qx=q_ref[b, :, h * M_DQK:(h + 1) * M_DQK], kx=k_ref[b, :, h * M_DQK:(h + 1) * M_DQK],
                vx=v_ref[b, :, h * M_DV:(h + 1) * M_DV], og=og_ref[b, :, h * M_DV:(h + 1) * M_DV],
                normg=normg_ref[h]))

    results = []
    for wk in work:
        ig_col, ig_row, b_col, b_row = wk["ig_col"], wk["ig_row"], wk["b_col"], wk["b_row"]
        m_prev, C, nvec, qx, kx, vx = wk["m_prev"], wk["C"], wk["nvec"], wk["qx"], wk["kx"], wk["vx"]

        inter = b_col + m_prev
        dmat = jnp.where(causal, b_col - b_row + ig_row, NEG_BIG)
        mt = jnp.maximum(inter, jnp.max(dmat, axis=1, keepdims=True))
        w = jnp.exp(dmat - mt)
        sqk = _dot_nt(qx, kx) * w
        s_i = jnp.exp(inter - mt)
        num = s_i * _dot(qx, C.astype(BF16)) + _dot(sqk.astype(BF16), vx)
        nq = s_i * jnp.sum(qx.astype(F32) * nvec, axis=1, keepdims=True) + jnp.sum(sqk, axis=1, keepdims=True)
        hh = num / jnp.maximum(jnp.abs(nq), jnp.exp(-mt))

        bl = b_col[L - 1:L, :]
        g_col = bl - b_col + ig_col
        m_new = jnp.maximum(bl + m_prev, jnp.max(g_col, axis=0, keepdims=True))
        decay = jnp.exp(bl + m_prev - m_new)
        wg = jnp.exp(g_col - m_new)
        kw = kx.astype(F32) * wg
        c_new = decay * C + _dot(kw.T.astype(BF16), vx)
        n_new = decay * nvec + jnp.sum(kw, axis=0, keepdims=True)

        hn = hh * lax.rsqrt(jnp.mean(hh * hh, axis=1, keepdims=True) + RMS_EPS) * wk["normg"]
        out = (hn * jax.nn.sigmoid(wk["og"].astype(F32))).astype(o_ref.dtype)
        results.append((c_new, n_new, m_new, out))

    for wk, (c_new, n_new, m_new, out) in zip(work, results):
        st, b, h = wk["st"], wk["b"], wk["h"]
        c_ref[st] = c_new
        n_ref[st] = n_new
        m_ref[st] = jnp.broadcast_to(m_new, (1, LANES))
        o_ref[b, :, h * M_DV:(h + 1) * M_DV] = out


def _mlstm(proj, gates, gbias, norm_g, B, S):
    nc = S // M_CHUNK
    L = M_CHUNK
    proj3 = proj.reshape(B, S, PROJ_COLS)
    gates3 = gates.reshape(B, S, LANES)
    out = pl.pallas_call(
        _mlstm_kernel,
        grid=(nc,),
        in_specs=[
            pl.BlockSpec((B, L, M_QK_W), lambda c: (0, c, C_MQ // M_QK_W)),
            pl.BlockSpec((B, L, M_QK_W), lambda c: (0, c, C_MK // M_QK_W)),
            pl.BlockSpec((B, L, M_V_W), lambda c: (0, c, C_MV // M_V_W)),
            pl.BlockSpec((B, L, M_V_W), lambda c: (0, c, C_MO // M_V_W)),
            pl.BlockSpec((B, L, LANES), lambda c: (0, c, 0)),
            pl.BlockSpec((1, LANES), lambda c: (0, 0)),
            pl.BlockSpec((M_HEADS, 1, M_DV), lambda c: (0, 0, 0)),
        ],
        out_specs=pl.BlockSpec((B, L, M_V_W), lambda c: (0, c, 0)),
        out_shape=jax.ShapeDtypeStruct((B, S, M_V_W), BF16),
        scratch_shapes=[
            pltpu.VMEM((B * M_HEADS, M_DQK, M_DV), F32),
            pltpu.VMEM((B * M_HEADS, 1, M_DQK), F32),
            pltpu.VMEM((B * M_HEADS, 1, LANES), F32),
        ],
        compiler_params=_cparams(1),
        name="mlstm_chunkwise",
    )(proj3, proj3, proj3, proj3, gates3, gbias, norm_g)
    return out.reshape(B * S, M_V_W)


def _mix_kernel(attn_ref, mh_ref, ga_ref, gb_ref, x_ref, lng_ref, lnb_ref, wba_ref, wbm_ref, wout_ref,
                mg_ref, mb_ref, o_ref):
    a = _dot(attn_ref[...], wba_ref[...])
    m = _dot(mh_ref[...], wbm_ref[...])
    merged = jax.nn.sigmoid(ga_ref[...].astype(F32)) * a + jax.nn.sigmoid(gb_ref[...].astype(F32)) * m
    y = _dot(merged.astype(BF16), wout_ref[...])
    h0 = _layer_norm(x_ref[...], lng_ref[...], lnb_ref[...])
    x1 = _layer_norm(DEEPNORM_ALPHA * h0 + y, mg_ref[...], mb_ref[...])
    _store_rows(o_ref, x1)


def _mix(attn, mh, proj, x2, ln_g, ln_b, w_ba, w_bm, w_out, mix_g, mix_b, tm):
    T, D = x2.shape
    return pl.pallas_call(
        _mix_kernel,
        grid=(T // tm,),
        in_specs=[
            pl.BlockSpec((tm, ATTN_Q_W), lambda i: (i, 0)),
            pl.BlockSpec((tm, M_V_W), lambda i: (i, 0)),
            pl.BlockSpec((tm, D), lambda i: (i, C_GA // D_MODEL)),
            pl.BlockSpec((tm, D), lambda i: (i, C_GB // D_MODEL)),
            pl.BlockSpec((tm, D), lambda i: (i, 0)),
            _const_spec((1, D)), _const_spec((1, D)),
            _const_spec((ATTN_Q_W, D)), _const_spec((M_V_W, D)), _const_spec((D, D)),
            _const_spec((1, D)), _const_spec((1, D)),
        ],
        out_specs=pl.BlockSpec(_row_shape(tm), lambda i: (0, i, 0)),
        out_shape=jax.ShapeDtypeStruct(_row_shape(T), F32),
        compiler_params=_cparams(1),
        name="mix_outproj_ln",
    )(attn, mh, proj, proj, x2, ln_g, ln_b, w_ba, w_bm, w_out, mix_g, mix_b)


def _router_kernel(x_ref, wa_ref, wb_ref, br_ref, idx_ref, gw_ref, rank_ref, cnt_ref, carry_ref):
    i = pl.program_id(0)
    tm = x_ref.shape[1] // SUBLANES
    E = N_EXPERTS

    @pl.when(i == 0)
    def _():
        carry_ref[...] = jnp.zeros_like(carry_ref)

    x = _load_rows(x_ref)
    xh = x.astype(BF16)
    xl = (x - xh.astype(F32)).astype(BF16)
    c = _dot(xh, wa_ref[...]) + _dot(xl, wb_ref[...])
    logits_t = (c + pltpu.roll(c, E, 1)).T[:E]
    scores = jax.nn.sigmoid(logits_t)
    sel = scores + br_ref[...]
    erow = lax.broadcasted_iota(jnp.int32, (E, tm), 0).astype(F32)

    idx_rows, g_rows, hits = [], [], []
    for _ in range(TOP_K):
        mx = jnp.max(sel, axis=0, keepdims=True)
        am = jnp.min(jnp.where(sel == mx, erow, float(E)), axis=0, keepdims=True)
        hit = erow == am
        g_rows.append(jnp.sum(jnp.where(hit, scores, 0.0), axis=0, keepdims=True))
        sel = jnp.where(hit, -jnp.inf, sel)
        idx_rows.append(am)
        hits.append(hit)
    gsum = functools.reduce(lambda a, b: a + b, g_rows)
    chosen = functools.reduce(lambda a, b: a + b, [h.astype(F32) for h in hits])

    r_i = lax.broadcasted_iota(jnp.int32, (tm, tm), 0)
    c_i = lax.broadcasted_iota(jnp.int32, (tm, tm), 1)
    before = (r_i < c_i).astype(BF16)
    prefix = _dot(chosen.astype(BF16), before) + carry_ref[:, :1]
    rank_rows = [jnp.sum(jnp.where(h, prefix, 0.0), axis=0, keepdims=True) for h in hits]

    krow = lax.broadcasted_iota(jnp.int32, (TOP_K, tm), 0)

    def stack(rows):
        out = jnp.zeros((TOP_K, tm), F32)
        for k, r in enumerate(rows):
            out = jnp.where(krow == k, r, out)
        return out

    idx_ref[...] = stack(idx_rows).astype(jnp.int32)
    gw_ref[...] = stack(g_rows) / gsum * ROUTED_SCALE
    rank_ref[...] = stack(rank_rows).astype(jnp.int32)
    carry_ref[...] = carry_ref[...] + jnp.sum(chosen, axis=1, keepdims=True)
    cnt_ref[...] = carry_ref[...]


def _router(x1, w_a, w_b, b_col, tm):
    T, D = x1.shape[1] // SUBLANES, D_MODEL
    tok = lambda i: (0, i)
    return pl.pallas_call(
        _router_kernel,
        grid=(T // tm,),
        in_specs=[
            pl.BlockSpec(_row_shape(tm), lambda i: (0, i, 0)),
            _const_spec((D, 2 * N_EXPERTS)),
            _const_spec((D, 2 * N_EXPERTS)),
            _const_spec((N_EXPERTS, 1)),
        ],
        out_specs=[pl.BlockSpec((TOP_K, tm), tok), pl.BlockSpec((TOP_K, tm), tok), pl.BlockSpec((TOP_K, tm), tok),
                   pl.BlockSpec((N_EXPERTS, LANES), lambda i: (0, 0))],
        out_shape=[jax.ShapeDtypeStruct((TOP_K, T), jnp.int32), jax.ShapeDtypeStruct((TOP_K, T), F32),
                   jax.ShapeDtypeStruct((TOP_K, T), jnp.int32), jax.ShapeDtypeStruct((N_EXPERTS, LANES), F32)],
        scratch_shapes=[pltpu.VMEM((N_EXPERTS, LANES), F32)],
        compiler_params=_cparams(1),
        name="router_topk_rank",
    )(x1, w_a, w_b, b_col)


ZFILL_SIZES = tuple(2 ** b for b in reversed(range(MOE_BLOCK.bit_length() - 1)))
assert MOE_BLOCK & (MOE_BLOCK - 1) == 0


def _dispatch_kernel(zrow_ref, dest_hbm, xp_ref, xs_hbm, idx_smem, zbuf, isem, zsem, rsem):
    i = pl.program_id(0)
    nsteps = pl.num_programs(0)
    tm = xp_ref.shape[1] // SUBLANES
    n = tm * TOP_K

    def idx_copy(tile, slot):
        return pltpu.make_async_copy(dest_hbm.at[pl.ds(tile * n, n)],
                                     idx_smem.at[pl.ds(pl.multiple_of(slot * n, n), n)], isem.at[slot])

    def zero_copy(row, rows):
        return pltpu.make_async_copy(zbuf.at[:, pl.ds(0, rows * SUBLANES)], xs_hbm.at[:, _tile_rows(row, rows)], zsem)

    def zero_fill_plan():
        plan = []
        for e in range(N_EXPERTS):
            start = zrow_ref[e]
            rem = zrow_ref[N_EXPERTS + e] - start
            for rows in ZFILL_SIZES:
                take = rem >= rows
                plan.append((take, start, rows))
                start = jnp.where(take, start + rows, start)
                rem = jnp.where(take, rem - rows, rem)
        n_alloc = xs_hbm.shape[1] // SUBLANES
        for j in range(N_EXPERTS + 2):
            s = zrow_ref[2 * N_EXPERTS] + j * MOE_BLOCK
            plan.append((s < n_alloc, jnp.minimum(s, n_alloc - MOE_BLOCK), MOE_BLOCK))
        return plan

    @pl.when(i == 0)
    def _():
        idx_copy(0, 0).start()
        zbuf[...] = jnp.zeros_like(zbuf)
        plan = zero_fill_plan()
        for take, start, rows in plan:
            @pl.when(take)
            def _():
                zero_copy(start, rows).start()
        for take, start, rows in plan:
            @pl.when(take)
            def _():
                zero_copy(0, rows).wait()

    slot = i % 2
    idx_copy(i, slot).wait()

    @pl.when(i + 1 < nsteps)
    def _():
        idx_copy(i + 1, 1 - slot).start()

    base = slot * n

    def body(t, carry):
        for k in range(TOP_K):
            d = idx_smem[base + t * TOP_K + k]
            pltpu.make_async_copy(xp_ref.at[:, _tile_rows(t)], xs_hbm.at[:, _tile_rows(d)],
                                  rsem).start(priority=k % 2)
        return carry

    lax.fori_loop(0, tm, body, 0)
    for _ in range(TOP_K):
        pltpu.make_async_copy(xp_ref, xs_hbm.at[:, _tile_rows(0, tm)], rsem).wait()


def _dispatch(x1p, dest_flat, zrow, n_rows_alloc, tm):
    T = x1p.shape[1] // SUBLANES
    grid_spec = pltpu.PrefetchScalarGridSpec(
        num_scalar_prefetch=1,
        grid=(T // tm,),
        in_specs=[
            pl.BlockSpec(memory_space=pl.ANY),
            pl.BlockSpec(_row_shape(tm), lambda i, z: (0, i, 0)),
        ],
        out_specs=pl.BlockSpec(memory_space=pl.ANY),
        scratch_shapes=[
            pltpu.SMEM((2 * tm * TOP_K,), jnp.int32),
            pltpu.VMEM(_row_shape(MOE_BLOCK), F32),
            pltpu.SemaphoreType.DMA((2,)),
            pltpu.SemaphoreType.DMA(()),
            pltpu.SemaphoreType.DMA(()),
        ],
    )
    return pl.pallas_call(
        _dispatch_kernel,
        grid_spec=grid_spec,
        out_shape=jax.ShapeDtypeStruct(_row_shape(n_rows_alloc), F32),
        compiler_params=_cparams(1),
        name="moe_dispatch",
    )(zrow, dest_flat, x1p)


def _expert_kernel(be_ref, nused_ref, xs_ref, wg_ref, wu_ref, wd_ref, y_ref, wg_s, wu_s, wd_s):
    i = pl.program_id(0)
    used = i < nused_ref[0]
    new_expert = (i == 0) | (be_ref[i] != be_ref[jnp.maximum(i - 1, 0)])

    @pl.when(used & new_expert)
    def _():
        wg_s[...] = wg_ref[0].astype(BF16)
        wu_s[...] = wu_ref[0].astype(BF16)
        wd_s[...] = wd_ref[0].astype(BF16)

    @pl.when(used)
    def _():
        xb = _load_rows(xs_ref).astype(BF16)
        hb = jax.nn.silu(_dot(xb, wg_s[...])) * _dot(xb, wu_s[...])
        y = _dot(hb.astype(BF16), wd_s[...])
        _store_rows(y_ref, y)

    @pl.when(jnp.logical_not(used))
    def _():
        y_ref[...] = jnp.zeros_like(y_ref)


def _experts(xs, block_e, n_used, w_eg, w_eu, w_ed, n_blocks):
    D = D_MODEL
    last_used = lambda i, nu: jnp.minimum(i, nu[0] - 1)
    row_block = _row_shape(MOE_BLOCK)
    grid_spec = pltpu.PrefetchScalarGridSpec(
        num_scalar_prefetch=2,
        grid=(n_blocks,),
        in_specs=[
            pl.BlockSpec(row_block, lambda i, be, nu: (0, last_used(i, nu), 0)),
            pl.BlockSpec((1, D, D_EXPERT), lambda i, be, nu: (be[i], 0, 0)),
            pl.BlockSpec((1, D, D_EXPERT), lambda i, be, nu: (be[i], 0, 0)),
            pl.BlockSpec((1, D_EXPERT, D), lambda i, be, nu: (be[i], 0, 0)),
        ],
        out_specs=pl.BlockSpec(row_block, lambda i, be, nu: (0, i, 0)),
        scratch_shapes=[pltpu.VMEM((D, D_EXPERT), BF16), pltpu.VMEM((D, D_EXPERT), BF16),
                        pltpu.VMEM((D_EXPERT, D), BF16)],
    )
    return pl.pallas_call(
        _expert_kernel,
        grid_spec=grid_spec,
        out_shape=jax.ShapeDtypeStruct(_row_shape(n_blocks * MOE_BLOCK), F32),
        compiler_params=_cparams(1),
        name="expert_mlp",
    )(block_e, n_used, xs, w_eg, w_eu, w_ed)


def _combine_kernel(dest_hbm, y_hbm, gw_ref, o_ref, idx_smem, ybuf, isem, rsem):
    i = pl.program_id(0)
    nsteps = pl.num_programs(0)
    tm = o_ref.shape[0]
    n = tm * TOP_K

    def idx_copy(tile, slot):
        return pltpu.make_async_copy(dest_hbm.at[pl.ds(tile * n, n)],
                                     idx_smem.at[pl.ds(pl.multiple_of(slot * n, n), n)], isem.at[slot])

    def start_row_gathers(slot):
        base = slot * n

        def body(t, carry):
            for k in range(TOP_K):
                d = idx_smem[base + t * TOP_K + k]
                pltpu.make_async_copy(y_hbm.at[:, _tile_rows(d)], ybuf.at[slot, k, :, _tile_rows(t)],
                                      rsem.at[slot]).start(priority=k % 2)
            return carry

        lax.fori_loop(0, tm, body, 0)

    slot = i % 2
    nxt = 1 - slot

    @pl.when(i == 0)
    def _():
        idx_copy(0, 0).start()
        idx_copy(0, 0).wait()
        start_row_gathers(0)

        @pl.when(nsteps > 1)
        def _():
            idx_copy(1, 1).start()

    @pl.when(i + 1 < nsteps)
    def _():
        idx_copy(i + 1, nxt).wait()

        @pl.when(i + 2 < nsteps)
        def _():
            idx_copy(i + 2, slot).start()

        start_row_gathers(nxt)

    for k in range(TOP_K):
        pltpu.make_async_copy(y_hbm.at[:, _tile_rows(0, tm)], ybuf.at[slot, k], rsem.at[slot]).wait()
    gw = gw_ref[...]
    routed = jnp.zeros((tm, D_MODEL), F32)
    for k in range(TOP_K):
        routed = routed + gw[:, k:k + 1] * _load_rows(ybuf.at[slot, k])
    o_ref[...] = routed


def _combine(dest_flat, y, gw_tok, tm):
    T = gw_tok.shape[0]
    return pl.pallas_call(
        _combine_kernel,
        grid=(T // tm,),
        in_specs=[
            pl.BlockSpec(memory_space=pl.ANY),
            pl.BlockSpec(memory_space=pl.ANY),
            pl.BlockSpec((tm, TOP_K), lambda i: (i, 0)),
        ],
        out_specs=pl.BlockSpec((tm, D_MODEL), lambda i: (i, 0)),
        out_shape=jax.ShapeDtypeStruct((T, D_MODEL), F32),
        scratch_shapes=[
            pltpu.SMEM((2 * tm * TOP_K,), jnp.int32),
            pltpu.VMEM((2, TOP_K) + _row_shape(tm), F32),
            pltpu.SemaphoreType.DMA((2,)),
            pltpu.SemaphoreType.DMA((2,)),
        ],
        compiler_params=_cparams(1),
        name="moe_combine",
    )(dest_flat, y, gw_tok)


def _ffn_ple_kernel(x1_ref, yr_ref, p_ref, wsg_ref, wsu_ref, wsd_ref, fg_ref, fb_ref, wpp_ref, wpg_ref,
                    pg_ref, pb_ref, o_ref):
    x1 = _load_rows(x1_ref)
    xb = x1.astype(BF16)
    hs = jax.nn.silu(_dot(xb, wsg_ref[...])) * _dot(xb, wsu_ref[...])
    shared = _dot(hs.astype(BF16), wsd_ref[...])
    x2 = _layer_norm(DEEPNORM_ALPHA * x1 + (yr_ref[...] + shared), fg_ref[...], fb_ref[...])
    ple = _dot(p_ref[...].astype(BF16), wpp_ref[...]) * jax.nn.sigmoid(_dot(x2.astype(BF16), wpg_ref[...]))
    o_ref[...] = _layer_norm(DEEPNORM_ALPHA * x2 + ple, pg_ref[...], pb_ref[...])


def _ffn_ple(x1, y_routed, p2, w_sg, w_su, w_sd, ffn_g, ffn_b, w_pp, w_pg, ple_g, ple_b, tm):
    T, D = y_routed.shape
    return pl.pallas_call(
        _ffn_ple_kernel,
        grid=(T // tm,),
        in_specs=[
            pl.BlockSpec(_row_shape(tm), lambda i: (0, i, 0)),
            pl.BlockSpec((tm, D), lambda i: (i, 0)),
            pl.BlockSpec((tm, PLE_DIM), lambda i: (i, 0)),
            _const_spec((D, D_SHARED)), _const_spec((D, D_SHARED)), _const_spec((D_SHARED, D)),
            _const_spec((1, D)), _const_spec((1, D)),
            _const_spec((PLE_DIM, D)), _const_spec((D, D)),
            _const_spec((1, D)), _const_spec((1, D)),
        ],
        out_specs=pl.BlockSpec((tm, D), lambda i: (i, 0)),
        out_shape=jax.ShapeDtypeStruct((T, D), F32),
        compiler_params=_cparams(1),
        name="shared_ffn_ple_ln",
    )(x1, y_routed, p2, w_sg, w_su, w_sd, ffn_g, ffn_b, w_pp, w_pg, ple_g, ple_b)


def _permute_w_in(w_in):
    D = w_in.shape[0]
    sizes = (ATTN_Q_W, ATTN_KV_W, ATTN_KV_W, M_QK_W, M_QK_W, M_V_W, M_V_W, M_HEADS, M_HEADS, D_MODEL, D_MODEL)
    offs = np.concatenate([[0], np.cumsum(sizes)])
    seg = [w_in[:, offs[k]:offs[k + 1]] for k in range(len(sizes))]
    aq, ak, av, mq, mk, mv, mo, mi, mf, ga, gb = seg
    aq = aq * (HEAD_DIM ** -0.5)
    mk = mk * (M_DQK ** -0.5)
    gates = jnp.concatenate([mi, mf, jnp.zeros((D, LANES - 2 * M_HEADS), w_in.dtype)], axis=1)
    return jnp.concatenate([ga, gb, aq, mv, mo, mq, mk, ak, av], axis=1).astype(BF16), gates.astype(BF16)


def _route_layout(idx_t, rank_t, counts, T):
    n_pairs = T * TOP_K
    n_blocks = -(-n_pairs // MOE_BLOCK) + N_EXPERTS
    padded = (counts + MOE_BLOCK - 1) // MOE_BLOCK * MOE_BLOCK
    ends = jnp.cumsum(padded)
    pstarts = ends - padded
    onehot = idx_t[:, :, None] == jnp.arange(N_EXPERTS, dtype=jnp.int32)
    dest_t = jnp.sum(jnp.where(onehot, pstarts.astype(jnp.int32), 0), axis=-1) + rank_t
    dest_flat = dest_t.T.reshape(-1)
    block_row = jnp.arange(n_blocks, dtype=jnp.int32) * MOE_BLOCK
    block_e = jnp.minimum(jnp.sum(ends[None, :] <= block_row[:, None], axis=1), N_EXPERTS - 1).astype(jnp.int32)
    n_used = (ends[-1] // MOE_BLOCK).astype(jnp.int32).reshape(1)
    zrow = jnp.concatenate([pstarts + counts, ends, ends[-1:]]).astype(jnp.int32)
    return dest_flat, block_e, n_used, zrow, n_blocks


def kernel(x, p, ln_in_g, ln_in_b, w_in, attn_sinks, mlstm_b_i, mlstm_b_f, mlstm_norm_g, w_branch_attn, w_branch_mlstm, w_out, ln_mix_g, ln_mix_b, w_router, b_router, w_exp_gate, w_exp_up, w_exp_down, w_sh_gate, w_sh_up, w_sh_down, ln_ffn_g, ln_ffn_b, w_ple_proj, w_ple_gate, ln_ple_g, ln_ple_b):
    B, S, D = x.shape
    T = B * S
    assert D == D_MODEL and S % WINDOW == 0 and S % M_CHUNK == 0 and w_in.shape[0] == DEPTH
    x2 = x.reshape(T, D)
    row = lambda v: v.reshape(1, -1).astype(F32)

    tm_proj = min(1024, T)
    tm_mix = min(256, T)
    tm_router = min(512, T)
    tm_dispatch = min(512, T)
    tm_combine = min(256, T)
    tm_ffn = min(256, T)

    w_perm, w_if = _permute_w_in(w_in[0])
    proj, gates = _in_projection(x2, row(ln_in_g), row(ln_in_b), w_perm, w_if, tm_proj)

    attn = _attention(proj, attn_sinks[0].astype(F32), B, S)

    gbias = jnp.concatenate([mlstm_b_i[0], mlstm_b_f[0], jnp.zeros((LANES - 2 * M_HEADS,), F32)]).reshape(1, LANES)
    mh = _mlstm(proj, gates, gbias, mlstm_norm_g[0].reshape(M_HEADS, 1, M_DV).astype(F32), B, S)

    x1 = _mix(attn, mh, proj, x2, row(ln_in_g), row(ln_in_b),
              w_branch_attn[0].astype(BF16), w_branch_mlstm[0].astype(BF16), w_out[0].astype(BF16),
              row(ln_mix_g[0]), row(ln_mix_b[0]), tm_mix)

    wr = w_router[0].astype(F32)
    wr_hi = wr.astype(BF16)
    wr_lo = (wr - wr_hi.astype(F32)).astype(BF16)
    w_a = jnp.concatenate([wr_hi, wr_lo], axis=1)
    w_b = jnp.concatenate([jnp.zeros_like(wr_hi), wr_hi], axis=1)
    idx_t, gw_t, rank_t, counts_f = _router(x1, w_a, w_b, b_router[0].reshape(N_EXPERTS, 1).astype(F32), tm_router)

    counts = counts_f[:, 0].astype(jnp.int32)
    dest_flat, block_e, n_used, zrow, n_blocks = _route_layout(idx_t, rank_t, counts, T)
    xs = _dispatch(x1, dest_flat, zrow, n_blocks * MOE_BLOCK, tm_dispatch)
    y = _experts(xs, block_e, n_used, w_exp_gate[0], w_exp_up[0], w_exp_down[0], n_blocks)

    y_routed = _combine(dest_flat, y, gw_t.T, tm_combine)
    out = _ffn_ple(x1, y_routed, p[0].reshape(T, PLE_DIM),
                   w_sh_gate[0].astype(BF16), w_sh_up[0].astype(BF16), w_sh_down[0].astype(BF16),
                   row(ln_ffn_g[0]), row(ln_ffn_b[0]),
                   w_ple_proj[0].astype(BF16), w_ple_gate[0].astype(BF16),
                   row(ln_ple_g[0]), row(ln_ple_b[0]), tm_ffn)
    return out.reshape(B, S, D)
```

```python
import functools

import numpy as np
import jax
import jax.numpy as jnp
from jax import lax
from jax.experimental import pallas as pl
from jax.experimental.pallas import tpu as pltpu

F32 = jnp.float32
BF16 = jnp.bfloat16

D_MODEL = 2048
PLE_DIM = 256
HEAD_DIM = 64
N_HEADS = 16
N_KV_HEADS = 2
GQA_GROUP = N_HEADS // N_KV_HEADS
WINDOW = 128
M_HEADS = 4
M_DV = 256
M_DQK = 128
M_CHUNK = 256
GATE_SOFTCAP = 15.0
N_EXPERTS = 64
TOP_K = 8
D_EXPERT = 512
D_SHARED = 512
ROUTED_SCALE = 2.5
MOE_BLOCK = 512
LN_EPS = 1e-5
RMS_EPS = 1e-6
DEPTH = 1
DEEPNORM_ALPHA = (2.0 * DEPTH) ** 0.25

ATTN_Q_W = N_HEADS * HEAD_DIM
ATTN_KV_W = N_KV_HEADS * HEAD_DIM
M_QK_W = M_HEADS * M_DQK
M_V_W = M_HEADS * M_DV

LANES = 128
NEG_BIG = -1e30

C_GA = 0
C_GB = C_GA + D_MODEL
C_AQ = C_GB + D_MODEL
C_MV = C_AQ + ATTN_Q_W
C_MO = C_MV + M_V_W
C_MQ = C_MO + M_V_W
C_MK = C_MQ + M_QK_W
C_AK = C_MK + M_QK_W
C_AV = C_AK + ATTN_KV_W
PROJ_COLS = C_AV + ATTN_KV_W
PROJ_TN = 1408
assert PROJ_COLS % PROJ_TN == 0 and PROJ_TN % LANES == 0

SUBLANES = 8
ROW_PLANES = D_MODEL // (SUBLANES * LANES)
assert ROW_PLANES * SUBLANES * LANES == D_MODEL

VMEM_LIMIT = 56 * 1024 * 1024


def _cparams(n_axes):
    return pltpu.CompilerParams(dimension_semantics=("arbitrary",) * n_axes, vmem_limit_bytes=VMEM_LIMIT)


def _const_spec(shape):
    nd = len(shape)
    return pl.BlockSpec(shape, lambda *_: (0,) * nd, pipeline_mode=pl.Buffered(1))


def _layer_norm(x, g, b):
    mu = jnp.mean(x, axis=-1, keepdims=True)
    xc = x - mu
    var = jnp.mean(xc * xc, axis=-1, keepdims=True)
    return xc * lax.rsqrt(var + LN_EPS) * g + b


def _dot(a, b):
    return jnp.dot(a, b, preferred_element_type=F32)


def _dot_nt(a, b):
    return lax.dot_general(a, b, (((1,), (1,)), ((), ())), preferred_element_type=F32)


def _row_shape(rows):
    return (ROW_PLANES, rows * SUBLANES, LANES)


def _store_rows(ref, val):
    rows = val.shape[0]
    for h in range(ROW_PLANES):
        for c in range(SUBLANES):
            col = (h * SUBLANES + c) * LANES
            ref[h, pl.ds(c, rows, stride=SUBLANES), :] = val[:, col:col + LANES]


def _load_rows(ref):
    rows = ref.shape[1] // SUBLANES
    return jnp.concatenate([ref[h, pl.ds(c, rows, stride=SUBLANES), :]
                            for h in range(ROW_PLANES) for c in range(SUBLANES)], axis=1)


def _tile_rows(r, rows=1):
    return pl.ds(pl.multiple_of(r * SUBLANES, SUBLANES), rows * SUBLANES)


def _inproj_kernel(x_ref, g_ref, b_ref, w_ref, wif_ref, o_ref, og_ref, xn_ref):
    j = pl.program_id(1)

    @pl.when(j == 0)
    def _():
        xn = _layer_norm(x_ref[...], g_ref[...], b_ref[...]).astype(BF16)
        xn_ref[...] = xn
        og_ref[...] = _dot(xn, wif_ref[...])

    o_ref[...] = _dot(xn_ref[...], w_ref[...]).astype(o_ref.dtype)


def _in_projection(x2, ln_g, ln_b, w_perm, w_if, tm):
    T, D = x2.shape
    return pl.pallas_call(
        _inproj_kernel,
        grid=(T // tm, PROJ_COLS // PROJ_TN),
        in_specs=[
            pl.BlockSpec((tm, D), lambda i, j: (i, 0)),
            pl.BlockSpec((1, D), lambda i, j: (0, 0)),
            pl.BlockSpec((1, D), lambda i, j: (0, 0)),
            pl.BlockSpec((D, PROJ_TN), lambda i, j: (0, j)),
            pl.BlockSpec((D, LANES), lambda i, j: (0, 0)),
        ],
        out_specs=[
            pl.BlockSpec((tm, PROJ_TN), lambda i, j: (i, j)),
            pl.BlockSpec((tm, LANES), lambda i, j: (i, 0)),
        ],
        out_shape=[
            jax.ShapeDtypeStruct((T, PROJ_COLS), BF16),
            jax.ShapeDtypeStruct((T, LANES), F32),
        ],
        scratch_shapes=[pltpu.VMEM((tm, D), BF16)],
        compiler_params=_cparams(2),
        name="ln_inproj",
    )(x2, ln_g, ln_b, w_perm, w_if)


def _alibi_slope(h):
    return float(2.0 ** (-8.0 / N_HEADS * (h + 1)))


def _attn_kernel(sink_ref, q_ref, kp_ref, kc_ref, vp_ref, vc_ref, o_ref, s_ref, p_ref, inv_ref, *, blocks_per_seq):
    n = pl.program_id(0) % blocks_per_seq
    W = WINDOW
    qi = lax.broadcasted_iota(jnp.int32, (W, 2 * W), 0)
    kj = lax.broadcasted_iota(jnp.int32, (W, 2 * W), 1)
    dist = qi - kj + W
    valid = (dist >= 0) & (dist < W) & ((kj >= W) | (n > 0))
    neg_dist = jnp.where(valid, -dist.astype(F32), NEG_BIG)

    lane = lax.broadcasted_iota(jnp.int32, (2 * W, LANES), 1)
    lo = lane < HEAD_DIM

    def dup_halves(prev_ref, cur_ref, g):
        a = jnp.concatenate([prev_ref[...], cur_ref[...]], axis=0).astype(F32)
        r = pltpu.roll(a, HEAD_DIM, 1)
        return jnp.where(lo, a, r) if g == 0 else jnp.where(lo, r, a)

    n_pairs = N_HEADS // 2
    vbds = []
    for g in range(N_KV_HEADS):
        kd = dup_halves(kp_ref, kc_ref, g)
        vd = dup_halves(vp_ref, vc_ref, g)
        zero = jnp.zeros_like(kd)
        kbd = jnp.concatenate([jnp.where(lo, kd, zero), jnp.where(lo, zero, kd)], axis=0).astype(BF16)
        vbds.append(jnp.concatenate([jnp.where(lo, vd, zero), jnp.where(lo, zero, vd)], axis=0).astype(BF16))
        for pp in range(GQA_GROUP // 2):
            pair = g * (GQA_GROUP // 2) + pp
            q2 = q_ref[:, pair * LANES:(pair + 1) * LANES]
            s_ref[pair] = _dot_nt(q2, kbd)

    for pair in range(n_pairs):
        s = s_ref[pair]
        ps = []
        for half, h in ((0, 2 * pair), (1, 2 * pair + 1)):
            sh = s[:, half * 2 * W:(half + 1) * 2 * W] + _alibi_slope(h) * neg_dist
            sink = sink_ref[h]
            m = jnp.maximum(jnp.max(sh, axis=1, keepdims=True), sink)
            p = jnp.exp(sh - m)
            l = jnp.sum(p, axis=1, keepdims=True) + jnp.exp(sink - m)
            ps.append(p)
            inv_ref[pair, :, half * HEAD_DIM:(half + 1) * HEAD_DIM] = jnp.broadcast_to(1.0 / l, (W, HEAD_DIM))
        p_ref[pair] = jnp.concatenate(ps, axis=1).astype(BF16)

    for pair in range(n_pairs):
        o2 = _dot(p_ref[pair], vbds[pair // (GQA_GROUP // 2)])
        o_ref[:, pair * LANES:(pair + 1) * LANES] = (o2 * inv_ref[pair]).astype(o_ref.dtype)


def _attention(proj, sinks, B, S):
    T = B * S
    nb = S // WINDOW
    kcol = C_AK // LANES
    vcol = C_AV // LANES
    kern = functools.partial(_attn_kernel, blocks_per_seq=nb)
    grid_spec = pltpu.PrefetchScalarGridSpec(
        num_scalar_prefetch=1,
        grid=(T // WINDOW,),
        in_specs=[
            pl.BlockSpec((WINDOW, ATTN_Q_W), lambda i, s: (i, C_AQ // ATTN_Q_W)),
            pl.BlockSpec((WINDOW, LANES), lambda i, s: (jnp.maximum(i - 1, 0), kcol)),
            pl.BlockSpec((WINDOW, LANES), lambda i, s: (i, kcol)),
            pl.BlockSpec((WINDOW, LANES), lambda i, s: (jnp.maximum(i - 1, 0), vcol)),
            pl.BlockSpec((WINDOW, LANES), lambda i, s: (i, vcol)),
        ],
        out_specs=pl.BlockSpec((WINDOW, ATTN_Q_W), lambda i, s: (i, 0)),
        scratch_shapes=[
            pltpu.VMEM((N_HEADS // 2, WINDOW, 4 * WINDOW), F32),
            pltpu.VMEM((N_HEADS // 2, WINDOW, 4 * WINDOW), BF16),
            pltpu.VMEM((N_HEADS // 2, WINDOW, LANES), F32),
        ],
    )
    return pl.pallas_call(
        kern,
        grid_spec=grid_spec,
        out_shape=jax.ShapeDtypeStruct((T, ATTN_Q_W), BF16),
        compiler_params=_cparams(1),
        name="swa_attention",
    )(sinks, proj, proj, proj, proj, proj)


def _soft_cap(z):
    return GATE_SOFTCAP * jnp.tanh(z / GATE_SOFTCAP)


def _mlstm_kernel(q_ref, k_ref, v_ref, og_ref, gates_ref, gbias_ref, normg_ref, o_ref, c_ref, n_ref, m_ref):
    L = M_CHUNK
    n_seq = q_ref.shape[0]

    @pl.when(pl.program_id(0) == 0)
    def _():
        c_ref[...] = jnp.zeros_like(c_ref)
        n_ref[...] = jnp.zeros_like(n_ref)
        m_ref[...] = jnp.zeros_like(m_ref)

    ti = lax.broadcasted_iota(jnp.int32, (L, L), 0)
    si = lax.broadcasted_iota(jnp.int32, (L, L), 1)
    causal = si <= ti
    tril = causal.astype(F32)

    work = []
    for b in range(n_seq):
        sc = _soft_cap(gates_ref[b] + gbias_ref[...])
        lf = jax.nn.log_sigmoid(sc)
        bcum = jnp.dot(tril, lf, precision=lax.Precision.HIGHEST, preferred_element_type=F32)
        sc_t = sc.T
        bcum_t = bcum.T
        for h in range(M_HEADS):
            st = b * M_HEADS + h
            work.append(dict(
                b=b, h=h, st=st,
                ig_col=sc[:, h:h + 1], ig_row=sc_t[h:h + 1, :],
                b_col=bcum[:, M_HEADS + h:M_HEADS + h + 1], b_row=bcum_t[M_HEADS + h:M_HEADS + h + 1, :],
                m_prev=m_ref[st][:, :1], C=c_ref[st], nvec=n_ref[st],
                qx=q_ref[b, :, h * M_DQK:(h + 1) * M_DQK], kx=k_ref[b, :, h * M_DQK:(h + 1) * M_DQK],
                vx=v_ref[b, :, h * M_DV:(h + 1) * M_DV], og=og_ref[b, :, h * M_DV:(h + 1) * M_DV],
                normg=normg_ref[h]))

    results = []
    for wk in work:
        ig_col, ig_row, b_col, b_row = wk["ig_col"], wk["ig_row"], wk["b_col"], wk["b_row"]
        m_prev, C, nvec, qx, kx, vx = wk["m_prev"], wk["C"], wk["nvec"], wk["qx"], wk["kx"], wk["vx"]

        inter = b_col + m_prev
        dmat = jnp.where(causal, b_col - b_row + ig_row, NEG_BIG)
        mt = jnp.maximum(inter, jnp.max(dmat, axis=1, keepdims=True))
        w = jnp.exp(dmat - mt)
        sqk = _dot_nt(qx, kx) * w
        s_i = jnp.exp(inter - mt)
        num = s_i * _dot(qx, C.astype(BF16)) + _dot(sqk.astype(BF16), vx)
        nq = s_i * jnp.sum(qx.astype(F32) * nvec, axis=1, keepdims=True) + jnp.sum(sqk, axis=1, keepdims=True)
        hh = num / jnp.maximum(jnp.abs(nq), jnp.exp(-mt))

        bl = b_col[L - 1:L, :]
        g_col = bl - b_col + ig_col
        m_new = jnp.maximum(bl + m_prev, jnp.max(g_col, axis=0, keepdims=True))
        decay = jnp.exp(bl + m_prev - m_new)
        wg = jnp.exp(g_col - m_new)
        kw = kx.astype(F32) * wg
        c_new = decay * C + _dot(kw.T.astype(BF16), vx)
        n_new = decay * nvec + jnp.sum(kw, axis=0, keepdims=True)

        hn = hh * lax.rsqrt(jnp.mean(hh * hh, axis=1, keepdims=True) + RMS_EPS) * wk["normg"]
        out = (hn * jax.nn.sigmoid(wk["og"].astype(F32))).astype(o_ref.dtype)
        results.append((c_new, n_new, m_new, out))

    for wk, (c_new, n_new, m_new, out) in zip(work, results):
        st, b, h = wk["st"], wk["b"], wk["h"]
        c_ref[st] = c_new
        n_ref[st] = n_new
        m_ref[st] = jnp.broadcast_to(m_new, (1, LANES))
        o_ref[b, :, h * M_DV:(h + 1) * M_DV] = out


def _mlstm(proj, gates, gbias, norm_g, B, S):
    nc = S // M_CHUNK
    L = M_CHUNK
    proj3 = proj.reshape(B, S, PROJ_COLS)
    gates3 = gates.reshape(B, S, LANES)
    out = pl.pallas_call(
        _mlstm_kernel,
        grid=(nc,),
        in_specs=[
            pl.BlockSpec((B, L, M_QK_W), lambda c: (0, c, C_MQ // M_QK_W)),
            pl.BlockSpec((B, L, M_QK_W), lambda c: (0, c, C_MK // M_QK_W)),
            pl.BlockSpec((B, L, M_V_W), lambda c: (0, c, C_MV // M_V_W)),
            pl.BlockSpec((B, L, M_V_W), lambda c: (0, c, C_MO // M_V_W)),
            pl.BlockSpec((B, L, LANES), lambda c: (0, c, 0)),
            pl.BlockSpec((1, LANES), lambda c: (0, 0)),
            pl.BlockSpec((M_HEADS, 1, M_DV), lambda c: (0, 0, 0)),
        ],
        out_specs=pl.BlockSpec((B, L, M_V_W), lambda c: (0, c, 0)),
        out_shape=jax.ShapeDtypeStruct((B, S, M_V_W), BF16),
        scratch_shapes=[
            pltpu.VMEM((B * M_HEADS, M_DQK, M_DV), F32),
            pltpu.VMEM((B * M_HEADS, 1, M_DQK), F32),
            pltpu.VMEM((B * M_HEADS, 1, LANES), F32),
        ],
        compiler_params=_cparams(1),
        name="mlstm_chunkwise",
    )(proj3, proj3, proj3, proj3, gates3, gbias, norm_g)
    return out.reshape(B * S, M_V_W)


def _mix_kernel(attn_ref, mh_ref, ga_ref, gb_ref, x_ref, lng_ref, lnb_ref, wba_ref, wbm_ref, wout_ref,
                mg_ref, mb_ref, o_ref):
    a = _dot(attn_ref[...], wba_ref[...])
    m = _dot(mh_ref[...], wbm_ref[...])
    merged = jax.nn.sigmoid(ga_ref[...].astype(F32)) * a + jax.nn.sigmoid(gb_ref[...].astype(F32)) * m
    y = _dot(merged.astype(BF16), wout_ref[...])
    h0 = _layer_norm(x_ref[...], lng_ref[...], lnb_ref[...])
    x1 = _layer_norm(DEEPNORM_ALPHA * h0 + y, mg_ref[...], mb_ref[...])
    _store_rows(o_ref, x1)


def _mix(attn, mh, proj, x2, ln_g, ln_b, w_ba, w_bm, w_out, mix_g, mix_b, tm):
    T, D = x2.shape
    return pl.pallas_call(
        _mix_kernel,
        grid=(T // tm,),
        in_specs=[
            pl.BlockSpec((tm, ATTN_Q_W), lambda i: (i, 0)),
            pl.BlockSpec((tm, M_V_W), lambda i: (i, 0)),
            pl.BlockSpec((tm, D), lambda i: (i, C_GA // D_MODEL)),
            pl.BlockSpec((tm, D), lambda i: (i, C_GB // D_MODEL)),
            pl.BlockSpec((tm, D), lambda i: (i, 0)),
            _const_spec((1, D)), _const_spec((1, D)),
            _const_spec((ATTN_Q_W, D)), _const_spec((M_V_W, D)), _const_spec((D, D)),
            _const_spec((1, D)), _const_spec((1, D)),
        ],
        out_specs=pl.BlockSpec(_row_shape(tm), lambda i: (0, i, 0)),
        out_shape=jax.ShapeDtypeStruct(_row_shape(T), F32),
        compiler_params=_cparams(1),
        name="mix_outproj_ln",
    )(attn, mh, proj, proj, x2, ln_g, ln_b, w_ba, w_bm, w_out, mix_g, mix_b)


def _router_kernel(x_ref, wa_ref, wb_ref, br_ref, idx_ref, gw_ref, rank_ref, cnt_ref, carry_ref):
    i = pl.program_id(0)
    tm = x_ref.shape[1] // SUBLANES
    E = N_EXPERTS

    @pl.when(i == 0)
    def _():
        carry_ref[...] = jnp.zeros_like(carry_ref)

    x = _load_rows(x_ref)
    xh = x.astype(BF16)
    xl = (x - xh.astype(F32)).astype(BF16)
    c = _dot(xh, wa_ref[...]) + _dot(xl, wb_ref[...])
    logits_t = (c + pltpu.roll(c, E, 1)).T[:E]
    scores = jax.nn.sigmoid(logits_t)
    sel = scores + br_ref[...]
    erow = lax.broadcasted_iota(jnp.int32, (E, tm), 0).astype(F32)

    idx_rows, g_rows, hits = [], [], []
    for _ in range(TOP_K):
        mx = jnp.max(sel, axis=0, keepdims=True)
        am = jnp.min(jnp.where(sel == mx, erow, float(E)), axis=0, keepdims=True)
        hit = erow == am
        g_rows.append(jnp.sum(jnp.where(hit, scores, 0.0), axis=0, keepdims=True))
        sel = jnp.where(hit, -jnp.inf, sel)
        idx_rows.append(am)
        hits.append(hit)
    gsum = functools.reduce(lambda a, b: a + b, g_rows)
    chosen = functools.reduce(lambda a, b: a + b, [h.astype(F32) for h in hits])

    r_i = lax.broadcasted_iota(jnp.int32, (tm, tm), 0)
    c_i = lax.broadcasted_iota(jnp.int32, (tm, tm), 1)
    before = (r_i < c_i).astype(BF16)
    prefix = _dot(chosen.astype(BF16), before) + carry_ref[:, :1]
    rank_rows = [jnp.sum(jnp.where(h, prefix, 0.0), axis=0, keepdims=True) for h in hits]

    krow = lax.broadcasted_iota(jnp.int32, (TOP_K, tm), 0)

    def stack(rows):
        out = jnp.zeros((TOP_K, tm), F32)
        for k, r in enumerate(rows):
            out = jnp.where(krow == k, r, out)
        return out

    idx_ref[...] = stack(idx_rows).astype(jnp.int32)
    gw_ref[...] = stack(g_rows) / gsum * ROUTED_SCALE
    rank_ref[...] = stack(rank_rows).astype(jnp.int32)
    carry_ref[...] = carry_ref[...] + jnp.sum(chosen, axis=1, keepdims=True)
    cnt_ref[...] = carry_ref[...]


def _router(x1, w_a, w_b, b_col, tm):
    T, D = x1.shape[1] // SUBLANES, D_MODEL
    tok = lambda i: (0, i)
    return pl.pallas_call(
        _router_kernel,
        grid=(T // tm,),
        in_specs=[
            pl.BlockSpec(_row_shape(tm), lambda i: (0, i, 0)),
            _const_spec((D, 2 * N_EXPERTS)),
            _const_spec((D, 2 * N_EXPERTS)),
            _const_spec((N_EXPERTS, 1)),
        ],
        out_specs=[pl.BlockSpec((TOP_K, tm), tok), pl.BlockSpec((TOP_K, tm), tok), pl.BlockSpec((TOP_K, tm), tok),
                   pl.BlockSpec((N_EXPERTS, LANES), lambda i: (0, 0))],
        out_shape=[jax.ShapeDtypeStruct((TOP_K, T), jnp.int32), jax.ShapeDtypeStruct((TOP_K, T), F32),
                   jax.ShapeDtypeStruct((TOP_K, T), jnp.int32), jax.ShapeDtypeStruct((N_EXPERTS, LANES), F32)],
        scratch_shapes=[pltpu.VMEM((N_EXPERTS, LANES), F32)],
        compiler_params=_cparams(1),
        name="router_topk_rank",
    )(x1, w_a, w_b, b_col)


ZFILL_SIZES = tuple(2 ** b for b in reversed(range(MOE_BLOCK.bit_length() - 1)))
assert MOE_BLOCK & (MOE_BLOCK - 1) == 0


def _dispatch_kernel(zrow_ref, dest_hbm, xp_ref, xs_hbm, idx_smem, zbuf, isem, zsem, rsem):
    i = pl.program_id(0)
    nsteps = pl.num_programs(0)
    tm = xp_ref.shape[1] // SUBLANES
    n = tm * TOP_K

    def idx_copy(tile, slot):
        return pltpu.make_async_copy(dest_hbm.at[pl.ds(tile * n, n)],
                                     idx_smem.at[pl.ds(pl.multiple_of(slot * n, n), n)], isem.at[slot])

    def zero_copy(row, rows):
        return pltpu.make_async_copy(zbuf.at[:, pl.ds(0, rows * SUBLANES)], xs_hbm.at[:, _tile_rows(row, rows)], zsem)

    def zero_fill_plan():
        plan = []
        for e in range(N_EXPERTS):
            start = zrow_ref[e]
            rem = zrow_ref[N_EXPERTS + e] - start
            for rows in ZFILL_SIZES:
                take = rem >= rows
                plan.append((take, start, rows))
                start = jnp.where(take, start + rows, start)
                rem = jnp.where(take, rem - rows, rem)
        n_alloc = xs_hbm.shape[1] // SUBLANES
        for j in range(N_EXPERTS + 2):
            s = zrow_ref[2 * N_EXPERTS] + j * MOE_BLOCK
            plan.append((s < n_alloc, jnp.minimum(s, n_alloc - MOE_BLOCK), MOE_BLOCK))
        return plan

    @pl.when(i == 0)
    def _():
        idx_copy(0, 0).start()
        zbuf[...] = jnp.zeros_like(zbuf)
        plan = zero_fill_plan()
        for take, start, rows in plan:
            @pl.when(take)
            def _():
                zero_copy(start, rows).start()
        for take, start, rows in plan:
            @pl.when(take)
            def _():
                zero_copy(0, rows).wait()

    slot = i % 2
    idx_copy(i, slot).wait()

    @pl.when(i + 1 < nsteps)
    def _():
        idx_copy(i + 1, 1 - slot).start()

    base = slot * n

    def body(t, carry):
        for k in range(TOP_K):
            d = idx_smem[base + t * TOP_K + k]
            pltpu.make_async_copy(xp_ref.at[:, _tile_rows(t)], xs_hbm.at[:, _tile_rows(d)],
                                  rsem).start(priority=k % 2)
        return carry

    lax.fori_loop(0, tm, body, 0)
    for _ in range(TOP_K):
        pltpu.make_async_copy(xp_ref, xs_hbm.at[:, _tile_rows(0, tm)], rsem).wait()


def _dispatch(x1p, dest_flat, zrow, n_rows_alloc, tm):
    T = x1p.shape[1] // SUBLANES
    grid_spec = pltpu.PrefetchScalarGridSpec(
        num_scalar_prefetch=1,
        grid=(T // tm,),
        in_specs=[
            pl.BlockSpec(memory_space=pl.ANY),
            pl.BlockSpec(_row_shape(tm), lambda i, z: (0, i, 0)),
        ],
        out_specs=pl.BlockSpec(memory_space=pl.ANY),
        scratch_shapes=[
            pltpu.SMEM((2 * tm * TOP_K,), jnp.int32),
            pltpu.VMEM(_row_shape(MOE_BLOCK), F32),
            pltpu.SemaphoreType.DMA((2,)),
            pltpu.SemaphoreType.DMA(()),
            pltpu.SemaphoreType.DMA(()),
        ],
    )
    return pl.pallas_call(
        _dispatch_kernel,
        grid_spec=grid_spec,
        out_shape=jax.ShapeDtypeStruct(_row_shape(n_rows_alloc), F32),
        compiler_params=_cparams(1),
        name="moe_dispatch",
    )(zrow, dest_flat, x1p)


def _expert_kernel(be_ref, nused_ref, xs_ref, wg_ref, wu_ref, wd_ref, y_ref, wg_s, wu_s, wd_s):
    i = pl.program_id(0)
    used = i < nused_ref[0]
    new_expert = (i == 0) | (be_ref[i] != be_ref[jnp.maximum(i - 1, 0)])

    @pl.when(used & new_expert)
    def _():
        wg_s[...] = wg_ref[0].astype(BF16)
        wu_s[...] = wu_ref[0].astype(BF16)
        wd_s[...] = wd_ref[0].astype(BF16)

    @pl.when(used)
    def _():
        xb = _load_rows(xs_ref).astype(BF16)
        rows = xb.shape[0]
        halves = [slice(0, rows // 2), slice(rows // 2, rows)]
        gu = [(_dot(xb[r], wg_s[...]), _dot(xb[r], wu_s[...])) for r in halves]
        hb = [(jax.nn.silu(g) * u).astype(BF16) for g, u in gu]
        y = jnp.concatenate([_dot(h, wd_s[...]) for h in hb], axis=0)
        _store_rows(y_ref, y)

    @pl.when(jnp.logical_not(used))
    def _():
        y_ref[...] = jnp.zeros_like(y_ref)


def _experts(xs, block_e, n_used, w_eg, w_eu, w_ed, n_blocks):
    D = D_MODEL
    last_used = lambda i, nu: jnp.minimum(i, nu[0] - 1)
    row_block = _row_shape(MOE_BLOCK)
    grid_spec = pltpu.PrefetchScalarGridSpec(
        num_scalar_prefetch=2,
        grid=(n_blocks,),
        in_specs=[
            pl.BlockSpec(row_block, lambda i, be, nu: (0, last_used(i, nu), 0)),
            pl.BlockSpec((1, D, D_EXPERT), lambda i, be, nu: (be[i], 0, 0)),
            pl.BlockSpec((1, D, D_EXPERT), lambda i, be, nu: (be[i], 0, 0)),
            pl.BlockSpec((1, D_EXPERT, D), lambda i, be, nu: (be[i], 0, 0)),
        ],
        out_specs=pl.BlockSpec(row_block, lambda i, be, nu: (0, i, 0)),
        scratch_shapes=[pltpu.VMEM((D, D_EXPERT), BF16), pltpu.VMEM((D, D_EXPERT), BF16),
                        pltpu.VMEM((D_EXPERT, D), BF16)],
    )
    return pl.pallas_call(
        _expert_kernel,
        grid_spec=grid_spec,
        out_shape=jax.ShapeDtypeStruct(_row_shape(n_blocks * MOE_BLOCK), F32),
        compiler_params=_cparams(1),
        name="expert_mlp",
    )(block_e, n_used, xs, w_eg, w_eu, w_ed)


def _combine_kernel(dest_hbm, y_hbm, gw_ref, o_ref, idx_smem, ybuf, isem, rsem):
    i = pl.program_id(0)
    nsteps = pl.num_programs(0)
    tm = o_ref.shape[0]
    n = tm * TOP_K

    def idx_copy(tile, slot):
        return pltpu.make_async_copy(dest_hbm.at[pl.ds(tile * n, n)],
                                     idx_smem.at[pl.ds(pl.multiple_of(slot * n, n), n)], isem.at[slot])

    def start_row_gathers(slot):
        base = slot * n

        def body(t, carry):
            for k in range(TOP_K):
                d = idx_smem[base + t * TOP_K + k]
                pltpu.make_async_copy(y_hbm.at[:, _tile_rows(d)], ybuf.at[slot, k, :, _tile_rows(t)],
                                      rsem.at[slot]).start(priority=k % 2)
            return carry

        lax.fori_loop(0, tm, body, 0)

    slot = i % 2
    nxt = 1 - slot

    @pl.when(i == 0)
    def _():
        idx_copy(0, 0).start()
        idx_copy(0, 0).wait()
        start_row_gathers(0)

        @pl.when(nsteps > 1)
        def _():
            idx_copy(1, 1).start()

    @pl.when(i + 1 < nsteps)
    def _():
        idx_copy(i + 1, nxt).wait()

        @pl.when(i + 2 < nsteps)
        def _():
            idx_copy(i + 2, slot).start()

        start_row_gathers(nxt)

    for k in range(TOP_K):
        pltpu.make_async_copy(y_hbm.at[:, _tile_rows(0, tm)], ybuf.at[slot, k], rsem.at[slot]).wait()
    gw = gw_ref[...]
    routed = jnp.zeros((tm, D_MODEL), F32)
    for k in range(TOP_K):
        routed = routed + gw[:, k:k + 1] * _load_rows(ybuf.at[slot, k])
    o_ref[...] = routed


def _combine(dest_flat, y, gw_tok, tm):
    T = gw_tok.shape[0]
    return pl.pallas_call(
        _combine_kernel,
        grid=(T // tm,),
        in_specs=[
            pl.BlockSpec(memory_space=pl.ANY),
            pl.BlockSpec(memory_space=pl.ANY),
            pl.BlockSpec((tm, TOP_K), lambda i: (i, 0)),
        ],
        out_specs=pl.BlockSpec((tm, D_MODEL), lambda i: (i, 0)),
        out_shape=jax.ShapeDtypeStruct((T, D_MODEL), F32),
        scratch_shapes=[
            pltpu.SMEM((2 * tm * TOP_K,), jnp.int32),
            pltpu.VMEM((2, TOP_K) + _row_shape(tm), F32),
            pltpu.SemaphoreType.DMA((2,)),
            pltpu.SemaphoreType.DMA((2,)),
        ],
        compiler_params=_cparams(1),
        name="moe_combine",
    )(dest_flat, y, gw_tok)


def _ffn_ple_kernel(x1_ref, yr_ref, p_ref, wsg_ref, wsu_ref, wsd_ref, fg_ref, fb_ref, wpp_ref, wpg_ref,
                    pg_ref, pb_ref, o_ref):
    x1 = _load_rows(x1_ref)
    xb = x1.astype(BF16)
    hs = jax.nn.silu(_dot(xb, wsg_ref[...])) * _dot(xb, wsu_ref[...])
    shared = _dot(hs.astype(BF16), wsd_ref[...])
    x2 = _layer_norm(DEEPNORM_ALPHA * x1 + (yr_ref[...] + shared), fg_ref[...], fb_ref[...])
    ple = _dot(p_ref[...].astype(BF16), wpp_ref[...]) * jax.nn.sigmoid(_dot(x2.astype(BF16), wpg_ref[...]))
    o_ref[...] = _layer_norm(DEEPNORM_ALPHA * x2 + ple, pg_ref[...], pb_ref[...])


def _ffn_ple(x1, y_routed, p2, w_sg, w_su, w_sd, ffn_g, ffn_b, w_pp, w_pg, ple_g, ple_b, tm):
    T, D = y_routed.shape
    return pl.pallas_call(
        _ffn_ple_kernel,
        grid=(T // tm,),
        in_specs=[
            pl.BlockSpec(_row_shape(tm), lambda i: (0, i, 0)),
            pl.BlockSpec((tm, D), lambda i: (i, 0)),
            pl.BlockSpec((tm, PLE_DIM), lambda i: (i, 0)),
            _const_spec((D, D_SHARED)), _const_spec((D, D_SHARED)), _const_spec((D_SHARED, D)),
            _const_spec((1, D)), _const_spec((1, D)),
            _const_spec((PLE_DIM, D)), _const_spec((D, D)),
            _const_spec((1, D)), _const_spec((1, D)),
        ],
        out_specs=pl.BlockSpec((tm, D), lambda i: (i, 0)),
        out_shape=jax.ShapeDtypeStruct((T, D), F32),
        compiler_params=_cparams(1),
        name="shared_ffn_ple_ln",
    )(x1, y_routed, p2, w_sg, w_su, w_sd, ffn_g, ffn_b, w_pp, w_pg, ple_g, ple_b)


def _permute_w_in(w_in):
    D = w_in.shape[0]
    sizes = (ATTN_Q_W, ATTN_KV_W, ATTN_KV_W, M_QK_W, M_QK_W, M_V_W, M_V_W, M_HEADS, M_HEADS, D_MODEL, D_MODEL)
    offs = np.concatenate([[0], np.cumsum(sizes)])
    seg = [w_in[:, offs[k]:offs[k + 1]] for k in range(len(sizes))]
    aq, ak, av, mq, mk, mv, mo, mi, mf, ga, gb = seg
    aq = aq * (HEAD_DIM ** -0.5)
    mk = mk * (M_DQK ** -0.5)
    gates = jnp.concatenate([mi, mf, jnp.zeros((D, LANES - 2 * M_HEADS), w_in.dtype)], axis=1)
    return jnp.concatenate([ga, gb, aq, mv, mo, mq, mk, ak, av], axis=1).astype(BF16), gates.astype(BF16)


def _route_layout(idx_t, rank_t, counts, T):
    n_pairs = T * TOP_K
    n_blocks = -(-n_pairs // MOE_BLOCK) + N_EXPERTS
    padded = (counts + MOE_BLOCK - 1) // MOE_BLOCK * MOE_BLOCK
    ends = jnp.cumsum(padded)
    pstarts = ends - padded
    onehot = idx_t[:, :, None] == jnp.arange(N_EXPERTS, dtype=jnp.int32)
    dest_t = jnp.sum(jnp.where(onehot, pstarts.astype(jnp.int32), 0), axis=-1) + rank_t
    dest_flat = dest_t.T.reshape(-1)
    block_row = jnp.arange(n_blocks, dtype=jnp.int32) * MOE_BLOCK
    block_e = jnp.minimum(jnp.sum(ends[None, :] <= block_row[:, None], axis=1), N_EXPERTS - 1).astype(jnp.int32)
    n_used = (ends[-1] // MOE_BLOCK).astype(jnp.int32).reshape(1)
    zrow = jnp.concatenate([pstarts + counts, ends, ends[-1:]]).astype(jnp.int32)
    return dest_flat, block_e, n_used, zrow, n_blocks


def kernel(x, p, ln_in_g, ln_in_b, w_in, attn_sinks, mlstm_b_i, mlstm_b_f, mlstm_norm_g, w_branch_attn, w_branch_mlstm, w_out, ln_mix_g, ln_mix_b, w_router, b_router, w_exp_gate, w_exp_up, w_exp_down, w_sh_gate, w_sh_up, w_sh_down, ln_ffn_g, ln_ffn_b, w_ple_proj, w_ple_gate, ln_ple_g, ln_ple_b):
    B, S, D = x.shape
    T = B * S
    assert D == D_MODEL and S % WINDOW == 0 and S % M_CHUNK == 0 and w_in.shape[0] == DEPTH
    x2 = x.reshape(T, D)
    row = lambda v: v.reshape(1, -1).astype(F32)

    tm_proj = min(1024, T)
    tm_mix = min(256, T)
    tm_router = min(512, T)
    tm_dispatch = min(1024, T)
    tm_combine = min(256, T)
    tm_ffn = min(256, T)

    w_perm, w_if = _permute_w_in(w_in[0])
    proj, gates = _in_projection(x2, row(ln_in_g), row(ln_in_b), w_perm, w_if, tm_proj)

    attn = _attention(proj, attn_sinks[0].astype(F32), B, S)

    gbias = jnp.concatenate([mlstm_b_i[0], mlstm_b_f[0], jnp.zeros((LANES - 2 * M_HEADS,), F32)]).reshape(1, LANES)
    mh = _mlstm(proj, gates, gbias, mlstm_norm_g[0].reshape(M_HEADS, 1, M_DV).astype(F32), B, S)

    x1 = _mix(attn, mh, proj, x2, row(ln_in_g), row(ln_in_b),
              w_branch_attn[0].astype(BF16), w_branch_mlstm[0].astype(BF16), w_out[0].astype(BF16),
              row(ln_mix_g[0]), row(ln_mix_b[0]), tm_mix)

    wr = w_router[0].astype(F32)
    wr_hi = wr.astype(BF16)
    wr_lo = (wr - wr_hi.astype(F32)).astype(BF16)
    w_a = jnp.concatenate([wr_hi, wr_lo], axis=1)
    w_b = jnp.concatenate([jnp.zeros_like(wr_hi), wr_hi], axis=1)
    idx_t, gw_t, rank_t, counts_f = _router(x1, w_a, w_b, b_router[0].reshape(N_EXPERTS, 1).astype(F32), tm_router)

    counts = counts_f[:, 0].astype(jnp.int32)
    dest_flat, block_e, n_used, zrow, n_blocks = _route_layout(idx_t, rank_t, counts, T)
    xs = _dispatch(x1, dest_flat, zrow, n_blocks * MOE_BLOCK, tm_dispatch)
    y = _experts(xs, block_e, n_used, w_exp_gate[0], w_exp_up[0], w_exp_down[0], n_blocks)

    y_routed = _combine(dest_flat, y, gw_t.T, tm_combine)
    out = _ffn_ple(x1, y_routed, p[0].reshape(T, PLE_DIM),
                   w_sh_gate[0].astype(BF16), w_sh_up[0].astype(BF16), w_sh_down[0].astype(BF16),
                   row(ln_ffn_g[0]), row(ln_ffn_b[0]),
                   w_ple_proj[0].astype(BF16), w_ple_gate[0].astype(BF16),
                   row(ln_ple_g[0]), row(ln_ple_b[0]), tm_ffn)
    return out.reshape(B, S, D)
```
